```python
import math
import jax, jax.numpy as jnp
from jax import lax
import numpy as np

D_MODEL = 4096
BATCH = 32
SEQ = 256
DEPTH = 2
DEC_BATCH = 4
DEC_SEQ = 1024
PAST_LEN = 512

GRID_W = 64
SSD_INNER = D_MODEL
SSD_HEAD_DIM = 64
SSD_HEADS = SSD_INNER // SSD_HEAD_DIM
SSD_GROUPS = 8
SSD_STATE = 128
SSD_CONV = 3
SSD_CHUNK = 128
SSD_XBC = SSD_INNER + 2 * SSD_GROUPS * SSD_STATE
SC_WIDTH = D_MODEL // 2
SC_CONV = 3
CF_WIDTH = D_MODEL // 2
CF_CONV = 31
N_BRANCH = 3
OFF_Z = 0
OFF_XBC = OFF_Z + SSD_INNER
OFF_DT = OFF_XBC + SSD_XBC
OFF_SC = OFF_DT + 2 * SSD_HEADS
OFF_CF = OFF_SC + 3 * SC_WIDTH
OFF_GATE = OFF_CF + 2 * CF_WIDTH
IN_COLS = OFF_GATE + N_BRANCH * D_MODEL
N_EXPERTS = 16
N_EXPERT_GROUPS = 4
EXPERTS_PER_GROUP = N_EXPERTS // N_EXPERT_GROUPS
TOP_K = 2
D_EXPERT = D_MODEL // 4
EPS = 1e-6

kernel_name = 'hybrid_ssd_conv_moe_diffusion_step'


def rmsnorm(x, g):
    xf = x.astype(jnp.float32)
    y = xf * lax.rsqrt(jnp.mean(xf * xf, axis=-1, keepdims=True) + EPS)
    return (y * g.astype(jnp.float32)).astype(x.dtype)


def layernorm(x, g, b):
    xf = x.astype(jnp.float32)
    mu = jnp.mean(xf, axis=-1, keepdims=True)
    xc = xf - mu
    var = jnp.mean(xc * xc, axis=-1, keepdims=True)
    y = xc * lax.rsqrt(var + EPS) * g.astype(jnp.float32) + b.astype(jnp.float32)
    return y.astype(x.dtype)


def dwconv_seq(u, w, bias):
    k, ch = w.shape
    y = lax.conv_general_dilated(u, w.reshape(k, 1, ch).astype(u.dtype), (1,), [(k // 2, k // 2)],
                                 dimension_numbers=('NWC', 'WIO', 'NWC'), feature_group_count=ch)
    return y + bias.astype(u.dtype)


def dwconv_grid(u, w, bias, rows, vertical):
    n, seq_len, ch = u.shape
    k = w.shape[0]
    ug = u.reshape(n, rows, GRID_W, ch)
    if vertical:
        kern = w.reshape(k, 1, 1, ch)
        pad = [(k // 2, k // 2), (0, 0)]
    else:
        kern = w.reshape(1, k, 1, ch)
        pad = [(0, 0), (k // 2, k // 2)]
    y = lax.conv_general_dilated(ug, kern.astype(u.dtype), (1, 1), pad,
                                 dimension_numbers=('NHWC', 'HWIO', 'NHWC'), feature_group_count=ch)
    return y.reshape(n, seq_len, ch) + bias.astype(u.dtype)


def ssd_chunked(x, dt, a, bmat, cmat, h0):
    b, seq_len, nh, hp = x.shape
    g, n = bmat.shape[2], bmat.shape[3]
    hg = nh // g
    q = SSD_CHUNK
    nc = seq_len // q
    f32 = jnp.float32
    xc = x.astype(f32).reshape(b, nc, q, g, hg, hp)
    dtc = dt.astype(f32).reshape(b, nc, q, g, hg)
    bc = bmat.astype(f32).reshape(b, nc, q, g, n)
    cc = cmat.astype(f32).reshape(b, nc, q, g, n)
    a_cum = jnp.cumsum(dtc * a.astype(f32).reshape(g, hg), axis=2)
    seg = a_cum[:, :, :, None] - a_cum[:, :, None, :]
    mask = jnp.tril(jnp.ones((q, q), dtype=bool))[:, :, None, None]
    decay = jnp.exp(jnp.where(mask, seg, -jnp.inf))
    cb = jnp.einsum('bctgn,bcsgn->bctsg', cc, bc)
    wts = cb[..., None] * decay * dtc[:, :, None]
    y_intra = jnp.einsum('bctsgh,bcsghp->bctghp', wts, xc)
    decay_end = jnp.exp(a_cum[:, :, -1:] - a_cum)
    states = jnp.einsum('bcsgh,bcsgn,bcsghp->bcghpn', decay_end * dtc, bc, xc)
    chunk_decay = jnp.exp(a_cum[:, :, -1])

    def step(h, inp):
        st, dec = inp
        return dec[..., None, None] * h + st, h

    h_init = h0.astype(f32).reshape(b, g, hg, hp, n)
    h_final, h_enter = lax.scan(step, h_init, (jnp.moveaxis(states, 1, 0), jnp.moveaxis(chunk_decay, 1, 0)))
    h_enter = jnp.moveaxis(h_enter, 0, 1)
    y_inter = jnp.einsum('bctgn,bcghpn->bctghp', cc, h_enter) * jnp.exp(a_cum)[..., None]
    y = (y_intra + y_inter).reshape(b, seq_len, nh, hp)
    return y, h_final.reshape(b, nh, hp, n)


def hybrid_mixer(h, p, l, h0, rows):
    b, seq_len, _ = h.shape
    f32 = jnp.float32
    proj = jnp.einsum('bld,dk->blk', h, p['w_in'][l])
    z = proj[..., OFF_Z:OFF_XBC]
    xbc = proj[..., OFF_XBC:OFF_DT]
    dt_raw = proj[..., OFF_DT:OFF_SC].reshape(b, seq_len, 2, SSD_HEADS)
    sc = proj[..., OFF_SC:OFF_CF]
    cf = proj[..., OFF_CF:OFF_GATE]
    gates = jax.nn.sigmoid(proj[..., OFF_GATE:].astype(f32)).astype(h.dtype).reshape(b, seq_len, N_BRANCH, D_MODEL)

    xbc = jax.nn.silu(dwconv_seq(xbc, p['ssd_conv_w'][l], p['ssd_conv_b'][l]))
    gn = SSD_GROUPS * SSD_STATE
    xs = xbc[..., :SSD_INNER].reshape(b, seq_len, SSD_HEADS, SSD_HEAD_DIM)
    bm = xbc[..., SSD_INNER:SSD_INNER + gn].reshape(b, seq_len, SSD_GROUPS, SSD_STATE)
    cm = xbc[..., SSD_INNER + gn:].reshape(b, seq_len, SSD_GROUPS, SSD_STATE)
    dt = jax.nn.softplus(dt_raw.astype(f32) + p['ssd_dt_bias'][l].astype(f32))
    a = -jnp.exp(p['ssd_a_log'][l].astype(f32))
    rev = lambda t: jnp.flip(t, axis=1)
    y_f, h_f = ssd_chunked(xs, dt[:, :, 0], a[0], bm, cm, h0[:, 0])
    y_b, h_b = ssd_chunked(rev(xs), rev(dt[:, :, 1]), a[1], rev(bm), rev(cm), h0[:, 1])
    y = y_f + rev(y_b) + p['ssd_d'][l].astype(f32)[:, None] * xs.astype(f32)
    y = y.reshape(b, seq_len, SSD_INNER) * jax.nn.silu(z.astype(f32))
    y = rmsnorm(y, p['ssd_norm_g'][l]).astype(h.dtype)
    o_ssd = jnp.einsum('blk,kd->bld', y, p['ssd_out'][l])

    bg, cg, hv = jnp.split(sc, 3, axis=-1)
    u = cg * hv
    if rows is None:
        u = dwconv_seq(u, p['sc_conv_w'][l], p['sc_conv_b'][l])
    else:
        u = dwconv_grid(u, p['sc_conv_w'][l], p['sc_conv_b'][l], rows, vertical=False)
    o_sc = jnp.einsum('blk,kd->bld', bg * u, p['sc_out'][l])

    ga, gb = jnp.split(cf, 2, axis=-1)
    u = ga * jax.nn.sigmoid(gb)
    if rows is None:
        u = dwconv_seq(u, p['cf_conv_w'][l], p['cf_conv_b'][l])
    else:
        u = dwconv_grid(u, p['cf_conv_w'][l], p['cf_conv_b'][l], rows, vertical=True)
    u = jax.nn.silu(layernorm(u, p['cf_ln_g'][l], p['cf_ln_b'][l]))
    o_cf = jnp.einsum('blk,kd->bld', u, p['cf_out'][l])

    merged = gates[:, :, 0] * o_ssd + gates[:, :, 1] * o_sc + gates[:, :, 2] * o_cf
    out = jnp.einsum('bld,de->ble', merged, p['w_o'][l])
    return out, jnp.stack([h_f, h_b], axis=1)


def grouped_moe(h, p, l):
    b, seq_len, d = h.shape
    f32 = jnp.float32
    t = h.reshape(b * seq_len, d)
    scores = jax.nn.sigmoid(jnp.einsum('td,de->te', t, p['router_w']).astype(f32))
    sel = scores + p['router_bias'].astype(f32)
    grp_score = lax.top_k(sel.reshape(-1, N_EXPERT_GROUPS, EXPERTS_PER_GROUP), TOP_K)[0].sum(-1)
    best = jnp.argmax(grp_score, axis=-1)
    in_grp = (jnp.arange(N_EXPERTS) // EXPERTS_PER_GROUP)[None, :] == best[:, None]
    _, idx = lax.top_k(jnp.where(in_grp, sel, -jnp.inf), TOP_K)
    w_sel = jnp.take_along_axis(scores, idx, axis=-1)
    w_sel = w_sel / jnp.sum(w_sel, axis=-1, keepdims=True)
    combine = jnp.sum(jax.nn.one_hot(idx, N_EXPERTS, dtype=f32) * w_sel[..., None], axis=1)
    gt = jnp.einsum('td,edf->tef', t, p['moe_w_gate'][l])
    up = jnp.einsum('td,edf->tef', t, p['moe_w_up'][l])
    act = jax.nn.silu(gt) * up * combine[..., None].astype(t.dtype)
    y = jnp.einsum('tef,efd->td', act, p['moe_w_down'][l])
    return y.reshape(b, seq_len, d)


def trunk(x, cond, h0_all, rows, p):
    states = []
    sc = jax.nn.silu(cond)
    for l in range(DEPTH):
        mod = (jnp.einsum('bd,dk->bk', sc, p['ada_w'][l]) + p['ada_b'][l])[:, None, :]
        sh1, s1, g1, sh2, s2, g2 = jnp.split(mod, 6, axis=-1)
        h = rmsnorm(x, p['norm1_g'][l]) * (1 + s1) + sh1
        out, st = hybrid_mixer(h, p, l, h0_all[:, l], rows)
        x = x + g1 * out
        h = rmsnorm(x, p['norm2_g'][l]) * (1 + s2) + sh2
        x = x + g2 * grouped_moe(h, p, l)
        states.append(st)
    return rmsnorm(x, p['final_g']), jnp.stack(states, axis=1)


def setup_inputs(seed: int = 0) -> dict:
    key = jax.random.key(seed)
    ks = jax.random.split(key, 40)
    nrm = lambda k, shape, scale: jax.random.normal(k, shape, jnp.float32) * scale
    dt0 = jnp.exp(jax.random.uniform(ks[10], (DEPTH, 2, SSD_HEADS), jnp.float32, math.log(1e-3), math.log(1e-1)))
    return {
        'x_prompt': nrm(ks[0], (BATCH, SEQ, D_MODEL), 1.0),
        'x_sample': nrm(ks[1], (DEC_BATCH, DEC_SEQ, D_MODEL), 1.0),
        'state_ssd': nrm(ks[2], (DEC_BATCH, DEPTH, 2, SSD_HEADS, SSD_HEAD_DIM, SSD_STATE), 0.5),
        'c': nrm(ks[3], (DEC_BATCH, D_MODEL), 1.0),
        'c_ctx': nrm(ks[4], (D_MODEL,), 1.0),
        'ada_w': nrm(ks[5], (DEPTH, D_MODEL, 6 * D_MODEL), 0.5 * D_MODEL ** -0.5),
        'ada_b': nrm(ks[6], (DEPTH, 6 * D_MODEL), 0.02),
        'norm1_g': 1.0 + nrm(ks[7], (DEPTH, D_MODEL), 0.02),
        'norm2_g': 1.0 + nrm(ks[8], (DEPTH, D_MODEL), 0.02),
        'w_in': nrm(ks[9], (DEPTH, D_MODEL, IN_COLS), D_MODEL ** -0.5),
        'ssd_conv_w': nrm(ks[11], (DEPTH, SSD_CONV, SSD_XBC), SSD_CONV ** -0.5),
        'ssd_conv_b': nrm(ks[12], (DEPTH, SSD_XBC), 0.02),
        'ssd_dt_bias': dt0 + jnp.log(-jnp.expm1(-dt0)),
        'ssd_a_log': jnp.log(jax.random.uniform(ks[13], (DEPTH, 2, SSD_HEADS), jnp.float32, 1.0, 16.0)),
        'ssd_d': 1.0 + nrm(ks[14], (DEPTH, SSD_HEADS), 0.1),
        'ssd_norm_g': 1.0 + nrm(ks[15], (DEPTH, SSD_INNER), 0.02),
        'ssd_out': nrm(ks[16], (DEPTH, SSD_INNER, D_MODEL), SSD_INNER ** -0.5),
        'sc_conv_w': nrm(ks[17], (DEPTH, SC_CONV, SC_WIDTH), SC_CONV ** -0.5),
        'sc_conv_b': nrm(ks[18], (DEPTH, SC_WIDTH), 0.02),
        'sc_out': nrm(ks[19], (DEPTH, SC_WIDTH, D_MODEL), SC_WIDTH ** -0.5),
        'cf_conv_w': nrm(ks[20], (DEPTH, CF_CONV, CF_WIDTH), CF_CONV ** -0.5),
        'cf_conv_b': nrm(ks[21], (DEPTH, CF_WIDTH), 0.02),
        'cf_ln_g': 1.0 + nrm(ks[22], (DEPTH, CF_WIDTH), 0.02),
        'cf_ln_b': nrm(ks[23], (DEPTH, CF_WIDTH), 0.02),
        'cf_out': nrm(ks[24], (DEPTH, CF_WIDTH, D_MODEL), CF_WIDTH ** -0.5),
        'w_o': nrm(ks[25], (DEPTH, D_MODEL, D_MODEL), D_MODEL ** -0.5),
        'router_w': nrm(ks[26], (D_MODEL, N_EXPERTS), D_MODEL ** -0.5),
        'router_bias': nrm(ks[27], (N_EXPERTS,), 0.01),
        'moe_w_gate': nrm(ks[28], (DEPTH, N_EXPERTS, D_MODEL, D_EXPERT), D_MODEL ** -0.5),
        'moe_w_up': nrm(ks[29], (DEPTH, N_EXPERTS, D_MODEL, D_EXPERT), D_MODEL ** -0.5),
        'moe_w_down': nrm(ks[30], (DEPTH, N_EXPERTS, D_EXPERT, D_MODEL), D_EXPERT ** -0.5),
        'final_g': 1.0 + nrm(ks[31], (D_MODEL,), 0.02),
    }


def reference(x_prompt, x_sample, state_ssd, c, c_ctx, ada_w, ada_b, norm1_g, norm2_g, w_in,
              ssd_conv_w, ssd_conv_b, ssd_dt_bias, ssd_a_log, ssd_d, ssd_norm_g, ssd_out,
              sc_conv_w, sc_conv_b, sc_out, cf_conv_w, cf_conv_b, cf_ln_g, cf_ln_b, cf_out, w_o,
              router_w, router_bias, moe_w_gate, moe_w_up, moe_w_down, final_g):
    p = dict(ada_w=ada_w, ada_b=ada_b, norm1_g=norm1_g, norm2_g=norm2_g, w_in=w_in,
             ssd_conv_w=ssd_conv_w, ssd_conv_b=ssd_conv_b, ssd_dt_bias=ssd_dt_bias, ssd_a_log=ssd_a_log,
             ssd_d=ssd_d, ssd_norm_g=ssd_norm_g, ssd_out=ssd_out, sc_conv_w=sc_conv_w, sc_conv_b=sc_conv_b,
             sc_out=sc_out, cf_conv_w=cf_conv_w, cf_conv_b=cf_conv_b, cf_ln_g=cf_ln_g, cf_ln_b=cf_ln_b,
             cf_out=cf_out, w_o=w_o, router_w=router_w, router_bias=router_bias, moe_w_gate=moe_w_gate,
             moe_w_up=moe_w_up, moe_w_down=moe_w_down, final_g=final_g)
    h0_ctx = jnp.zeros((x_prompt.shape[0], DEPTH, 2, SSD_HEADS, SSD_HEAD_DIM, SSD_STATE), jnp.float32)
    y_prompt, new_state_ssd = trunk(x_prompt, c_ctx[None, :], h0_ctx, None, p)
    grid_rows = x_sample.shape[1] // GRID_W
    y_sample, _ = trunk(x_sample, c, state_ssd, grid_rows, p)
    return (y_prompt, y_sample, new_state_ssd)
```

```python
import functools

import jax
import jax.numpy as jnp
import numpy as np
from jax import lax
from jax.experimental import pallas as pl
from jax.experimental.pallas import tpu as pltpu

F32 = jnp.float32
BF16 = jnp.bfloat16
I32 = jnp.int32
U32 = jnp.uint32
HIGHEST = lax.Precision.HIGHEST

EPS = 1e-6
GRID_W = 64
HEAD_DIM = 64
PAIR = 2 * HEAD_DIM
SSD_GROUPS = 8
SSD_STATE = 128
CHUNK = 128
N_EXPERT_GROUPS = 4
GROUP_SIZE = 4
MIB = 1024 * 1024
VMEM_BUDGET_MIB = 56


def _cparams(n_axes, vmem_mib):
    return pltpu.CompilerParams(
        dimension_semantics=("arbitrary",) * n_axes,
        vmem_limit_bytes=min(vmem_mib, VMEM_BUDGET_MIB) * MIB)


def _sigmoid(x):
    return 1.0 / (1.0 + jnp.exp(-x))


def _silu(x):
    return x * _sigmoid(x)


def _mod_row(start, t_ctx, dec_seq, n_dec):
    return jnp.where(start < t_ctx, n_dec, lax.div(start - t_ctx, dec_seq))


def _ada_kernel(c_ref, w_ref, b_ref, o_ref):
    s = _silu(c_ref[...])
    o_ref[...] = jnp.dot(s, w_ref[...], precision=HIGHEST,
                         preferred_element_type=F32) + b_ref[...]


def _ada_table(cond, ada_w, ada_b):
    depth, d, n = ada_w.shape
    r = cond.shape[0]
    tn = 512
    return pl.pallas_call(
        _ada_kernel,
        grid=(depth, n // tn),
        in_specs=[
            pl.BlockSpec((r, d), lambda l, j: (0, 0)),
            pl.BlockSpec((None, d, tn), lambda l, j: (l, 0, j)),
            pl.BlockSpec((None, 1, tn), lambda l, j: (l, 0, j)),
        ],
        out_specs=pl.BlockSpec((None, r, tn), lambda l, j: (l, 0, j)),
        out_shape=jax.ShapeDtypeStruct((depth, r, n), F32),
        compiler_params=_cparams(2, 32),
    )(cond, ada_w, ada_b.reshape(depth, 1, n))


def _normmod_kernel(x_ref, g_ref, sh_ref, s_ref, o_ref, *, tm, geo):
    row = _mod_row(pl.program_id(0) * tm, *geo)
    x = x_ref[...]
    y = x * lax.rsqrt(jnp.mean(x * x, axis=-1, keepdims=True) + EPS) * g_ref[...]
    o_ref[...] = (y * (1.0 + s_ref[pl.ds(row, 1), :]) + sh_ref[pl.ds(row, 1), :]).astype(BF16)


def _normmod(x, g, mod, l, geo):
    t, d = x.shape
    r = mod.shape[1]
    tm = 256
    return pl.pallas_call(
        functools.partial(_normmod_kernel, tm=tm, geo=geo),
        grid=(t // tm,),
        in_specs=[
            pl.BlockSpec((tm, d), lambda i: (i, 0)),
            pl.BlockSpec((1, d), lambda i: (0, 0)),
            pl.BlockSpec((None, r, d), lambda i: (l, 0, 0)),
            pl.BlockSpec((None, r, d), lambda i: (l, 0, 1)),
        ],
        out_specs=pl.BlockSpec((tm, d), lambda i: (i, 0)),
        out_shape=jax.ShapeDtypeStruct((t, d), BF16),
        compiler_params=_cparams(1, 32),
    )(x, g, mod, mod)


def _proj_kernel(*refs, n_dots, n_extra, epilogue):
    h_ref = refs[0]
    w_refs = refs[1:1 + n_dots]
    extra = refs[1 + n_dots:1 + n_dots + n_extra]
    o_ref = refs[1 + n_dots + n_extra]
    wb = refs[2 + n_dots + n_extra:]
    i = pl.program_id(1)

    @pl.when(i == 0)
    def _():
        for d in range(n_dots):
            wb[d][...] = w_refs[d][...].astype(BF16)

    h = h_ref[...]
    accs = [jnp.dot(h, wb[d][...], preferred_element_type=F32) for d in range(n_dots)]
    o_ref[...] = epilogue(accs, extra, i).astype(o_ref.dtype)


def _proj(h, w_in, l, col_offs, n_cols, tn, tm, epilogue, extra, extra_specs, out_dtype):
    t, k = h.shape
    n_dots = len(col_offs)

    def wspec(off):
        return pl.BlockSpec((pl.Squeezed(), pl.Element(k), pl.Element(tn)),
                            lambda j, i: (l, 0, pl.multiple_of(off + j * tn, 128)))

    out_bytes = jnp.dtype(out_dtype).itemsize
    vmem = (2 * tm * k * 2 + n_dots * k * tn * (2 * 4 + 2) + 2 * tm * tn * out_bytes) // MIB + 8
    return pl.pallas_call(
        functools.partial(_proj_kernel, n_dots=n_dots, n_extra=len(extra), epilogue=epilogue),
        grid=(n_cols // tn, t // tm),
        in_specs=[pl.BlockSpec((tm, k), lambda j, i: (i, 0))]
        + [wspec(off) for off in col_offs] + list(extra_specs),
        out_specs=pl.BlockSpec((tm, tn), lambda j, i: (i, j)),
        out_shape=jax.ShapeDtypeStruct((t, n_cols), out_dtype),
        scratch_shapes=[pltpu.VMEM((k, tn), BF16)] * n_dots,
        compiler_params=_cparams(2, vmem),
    )(h, *([w_in] * n_dots), *extra)


def _conv3_rows(u, w_ref, b_ref, period):
    rows = u.shape[0]
    pos = lax.broadcasted_iota(I32, u.shape, 0) & (period - 1)
    prev = jnp.where(pos == 0, 0.0, pltpu.roll(u, 1, 0))
    nxt = jnp.where(pos == period - 1, 0.0, pltpu.roll(u, rows - 1, 0))
    return w_ref[0:1, :] * prev + w_ref[1:2, :] * u + w_ref[2:3, :] * nxt + b_ref[...]


def _epi_identity(accs, extra, i):
    return accs[0]


def _epi_sigmoid(accs, extra, i):
    return _sigmoid(accs[0])


def _epi_xbc(accs, extra, i, *, tm, t_ctx, seq, dec_seq):
    period = jnp.where(i * tm < t_ctx, seq, dec_seq)
    return _silu(_conv3_rows(accs[0], extra[0], extra[1], period))


def _epi_dt(accs, extra, i):
    x = accs[0] + extra[0][...]
    return jnp.maximum(x, 0.0) + jnp.log1p(jnp.exp(-jnp.abs(x)))


def _epi_sc(accs, extra, i, *, tm, t_ctx, seq):
    bg, cg, hv = accs
    period = jnp.where(i * tm < t_ctx, seq, GRID_W)
    return bg * _conv3_rows(cg * hv, extra[0], extra[1], period)


def _epi_cf(accs, extra, i):
    return accs[0] * _sigmoid(accs[1])


TBL_BLOCK, TBL_FIRST, TBL_LAST, TBL_LATENT, TBL_H0, TBL_STATE, TBL_FIELDS = 0, 1, 2, 3, 4, 5, 6


def _ssd_kernel(tbl, *refs, reverse, fused, n_heads, n_steps):
    it = iter(refs)
    xs_ref, b_ref, c_ref, dt_ref, alog_ref, h0_ref = (next(it) for _ in range(6))
    if fused:
        yf_ref, z_ref, dskip_ref, ng_ref = (next(it) for _ in range(4))
    next(it)
    y_ref, st_ref, h_scr = next(it), next(it), next(it)
    ybuf = next(it) if fused else None

    step = pl.program_id(0)
    first = tbl[TBL_FIRST * n_steps + step] == 1
    latent = tbl[TBL_LATENT * n_steps + step] == 1

    @pl.when(first & latent)
    def _():
        h_scr[...] = h0_ref[...]

    @pl.when(first & jnp.logical_not(latent))
    def _():
        h_scr[...] = jnp.zeros(h_scr.shape, F32)

    q = CHUNK
    dt = dt_ref[...]
    dta = dt * (-jnp.exp(alog_ref[...]))
    r_i = lax.broadcasted_iota(I32, (q, q), 0)
    c_i = lax.broadcasted_iota(I32, (q, q), 1)
    tri = (c_i >= r_i) if reverse else (c_i <= r_i)
    acum = jnp.dot(tri.astype(F32), dta, precision=HIGHEST, preferred_element_type=F32)
    acum_t = acum.T
    last = acum[0:1, :] if reverse else acum[q - 1:q, :]
    expa = jnp.exp(acum)
    dend = jnp.exp(last - acum)
    cdec = jnp.exp(last)
    lane_lo = lax.broadcasted_iota(I32, (q, PAIR), 1) < HEAD_DIM
    row_lo = lax.broadcasted_iota(I32, (PAIR, SSD_STATE), 0) < HEAD_DIM
    d0 = n_heads if reverse else 0
    hg = n_heads // SSD_GROUPS
    nt = (((1,), (1,)), ((), ()))

    def col(a, hx):
        return jnp.where(lane_lo, a[:, hx:hx + 1], a[:, hx + 1:hx + 2])

    for g in range(SSD_GROUPS):
        cg = c_ref[:, g * SSD_STATE:(g + 1) * SSD_STATE].astype(BF16)
        bg = b_ref[:, g * SSD_STATE:(g + 1) * SSD_STATE].astype(BF16)
        cb = lax.dot_general(cg, bg, nt, preferred_element_type=F32)
        for jp in range(hg // 2):
            pair = (g * hg) // 2 + jp
            hx = d0 + 2 * pair

            def decay_w(hh):
                diff = acum[:, hh:hh + 1] - acum_t[hh:hh + 1, :]
                return (cb * jnp.exp(jnp.where(tri, diff, -jnp.inf))).astype(BF16)

            w = jnp.concatenate([decay_w(hx), decay_w(hx + 1)], axis=1)
            xdt = xs_ref[:, pair * PAIR:(pair + 1) * PAIR] * col(dt, hx)
            xbd = jnp.concatenate([jnp.where(lane_lo, xdt, 0.0),
                                   jnp.where(lane_lo, 0.0, xdt)], axis=0).astype(BF16)
            y = jnp.dot(w, xbd, preferred_element_type=F32)
            hp = h_scr[pair]
            y_in = lax.dot_general(cg, hp.astype(BF16), nt, preferred_element_type=F32)
            y = y + y_in * col(expa, hx)
            xw = (xdt * col(dend, hx)).T.astype(BF16)
            st = jnp.dot(xw, bg, preferred_element_type=F32)
            h_scr[pair] = jnp.where(row_lo, cdec[:, hx:hx + 1], cdec[:, hx + 1:hx + 2]) * hp + st
            if fused:
                ybuf[:, pair * PAIR:(pair + 1) * PAIR] = y
            else:
                y_ref[:, pair * PAIR:(pair + 1) * PAIR] = y

    @pl.when((tbl[TBL_LAST * n_steps + step] == 1) & jnp.logical_not(latent))
    def _():
        st_ref[...] = h_scr[...]

    if fused:
        yt = ybuf[...] + yf_ref[...] + dskip_ref[...] * xs_ref[...]
        yt = yt * _silu(z_ref[...])
        yn = yt * lax.rsqrt(jnp.mean(yt * yt, axis=-1, keepdims=True) + EPS) * ng_ref[...]
        y_ref[...] = yn.astype(BF16)


def _ssd_table(n_ctx, seq, n_dec, dec_seq, reverse):
    steps = []
    blk = 0
    for n_seq, length, latent in ((n_ctx, seq, 0), (n_dec, dec_seq, 1)):
        n_chunks = length // CHUNK
        for b in range(n_seq):
            for c in range(n_chunks):
                steps.append((blk + c, int(c == 0), int(c == n_chunks - 1), latent,
                              b if latent else 0, n_ctx - 1 if latent else b))
            blk += n_chunks
    if reverse:
        steps = [(s[0], s[2], s[1], s[3], s[4], s[5]) for s in reversed(steps)]
    return np.asarray(steps, np.int32).T.reshape(-1), len(steps)


def _ssd_pass(xbc, dt, alog, h0, state_buf, *, d, l, shape, reverse, fused_in=None):
    n_heads = d // HEAD_DIM
    gn = SSD_GROUPS * SSD_STATE
    direction = 1 if reverse else 0
    fused = fused_in is not None
    t = xbc.shape[0]
    table, n_steps = _ssd_table(*shape, reverse)

    def field(f):
        return lambda s, tbl: tbl[f * n_steps + s]

    blk = field(TBL_BLOCK)
    row_spec = pl.BlockSpec((CHUNK, d), lambda s, tbl: (blk(s, tbl), 0))
    vec_spec = pl.BlockSpec((1, d), lambda s, tbl: (0, 0))
    state_block = (None, None, None, n_heads // 2, PAIR, SSD_STATE)
    in_specs = [
        row_spec,
        pl.BlockSpec((CHUNK, gn), lambda s, tbl: (blk(s, tbl), d // gn)),
        pl.BlockSpec((CHUNK, gn), lambda s, tbl: (blk(s, tbl), d // gn + 1)),
        pl.BlockSpec((CHUNK, 2 * n_heads), lambda s, tbl: (blk(s, tbl), 0)),
        pl.BlockSpec((1, 2 * n_heads), lambda s, tbl: (0, 0)),
        pl.BlockSpec(state_block,
                     lambda s, tbl: (field(TBL_H0)(s, tbl), l, direction, 0, 0, 0)),
    ]
    args = [jnp.asarray(table), xbc, xbc, xbc, dt, alog, h0]
    if fused:
        in_specs += [row_spec, row_spec, vec_spec, vec_spec]
        args += list(fused_in)
    in_specs.append(pl.BlockSpec(memory_space=pl.ANY))
    args.append(state_buf)
    scratch = [pltpu.VMEM((n_heads // 2, PAIR, SSD_STATE), F32)]
    if fused:
        scratch.append(pltpu.VMEM((CHUNK, d), F32))
    return pl.pallas_call(
        functools.partial(_ssd_kernel, reverse=reverse, fused=fused, n_heads=n_heads,
                          n_steps=n_steps),
        grid_spec=pltpu.PrefetchScalarGridSpec(
            num_scalar_prefetch=1,
            grid=(n_steps,),
            in_specs=in_specs,
            out_specs=[
                row_spec,
                pl.BlockSpec(state_block,
                             lambda s, tbl: (field(TBL_STATE)(s, tbl), l, direction, 0, 0, 0)),
            ],
            scratch_shapes=scratch,
        ),
        out_shape=[jax.ShapeDtypeStruct((t, d), BF16 if fused else F32),
                   jax.ShapeDtypeStruct(state_buf.shape, F32)],
        input_output_aliases={len(args) - 1: 1},
        compiler_params=_cparams(1, 40),
    )(*args)


def _ln_swish_store(conv, g_ref, b_ref, o_ref):
    x = conv[...]
    mu = jnp.mean(x, axis=-1, keepdims=True)
    xc = x - mu
    var = jnp.mean(xc * xc, axis=-1, keepdims=True)
    y = xc * lax.rsqrt(var + EPS) * g_ref[...] + b_ref[...]
    o_ref[...] = _silu(y).astype(o_ref.dtype)


def _cfconv_kernel(u_ref, w_ref, b_ref, g_ref, be_ref, o_ref, pad, conv, *,
                   seq, block, width, k, n_ctx_blocks):
    hk = k // 2
    lead = 16
    rows = block // GRID_W
    i = pl.program_id(0)

    @pl.when(i < n_ctx_blocks)
    def _():
        pad[0:lead, :] = jnp.zeros((lead, width), F32)
        pad[lead + seq:2 * lead + seq, :] = jnp.zeros((lead, width), F32)
        for sq in range(block // seq):
            pad[lead:lead + seq, :] = u_ref[sq * seq:(sq + 1) * seq, :]

            def body(cbi, carry):
                ds = pl.ds(pl.multiple_of(cbi * 128, 128), 128)
                acc = jnp.zeros((seq, 128), F32)
                for tap in range(k):
                    acc = acc + w_ref[tap:tap + 1, ds] * pad[pl.ds(lead - hk + tap, seq), ds]
                conv[pl.ds(sq * seq, seq), ds] = acc + b_ref[:, ds]
                return carry

            lax.fori_loop(0, width // 128, body, 0)

    @pl.when(i >= n_ctx_blocks)
    def _():
        def body(cbi, carry):
            ds = pl.ds(pl.multiple_of(cbi * 128, 128), 128)
            for r in range(rows):
                acc = None
                for tap in range(k):
                    src = r + tap - hk
                    if 0 <= src < rows:
                        term = w_ref[tap:tap + 1, ds] * u_ref[pl.ds(src * GRID_W, GRID_W), ds]
                        acc = term if acc is None else acc + term
                conv[pl.ds(r * GRID_W, GRID_W), ds] = acc + b_ref[:, ds]
            return carry

        lax.fori_loop(0, width // 128, body, 0)

    _ln_swish_store(conv, g_ref, be_ref, o_ref)


def _cfconv(u, w, b, g, be, *, t_ctx, seq, dec_seq):
    t, width = u.shape
    k = w.shape[0]
    assert dec_seq % seq == 0 and t_ctx % dec_seq == 0 and k // 2 < 16
    vec = pl.BlockSpec((1, width), lambda i: (0, 0))
    return pl.pallas_call(
        functools.partial(_cfconv_kernel, seq=seq, block=dec_seq, width=width, k=k,
                          n_ctx_blocks=t_ctx // dec_seq),
        grid=(t // dec_seq,),
        in_specs=[pl.BlockSpec((dec_seq, width), lambda i: (i, 0)),
                  pl.BlockSpec((k, width), lambda i: (0, 0)), vec, vec, vec],
        out_specs=pl.BlockSpec((dec_seq, width), lambda i: (i, 0)),
        out_shape=jax.ShapeDtypeStruct((t, width), BF16),
        scratch_shapes=[pltpu.VMEM((seq + 32, width), F32), pltpu.VMEM((dec_seq, width), F32)],
        compiler_params=_cparams(1, 48),
    )(u, w, b, g, be)


def _merge_kernel(y_ref, v_ref, u_ref, g0_ref, g1_ref, g2_ref, w0_ref, w1_ref, w2_ref,
                  o_ref, wb0, wb1, wb2):
    @pl.when(pl.program_id(1) == 0)
    def _():
        wb0[...] = w0_ref[...].astype(BF16)
        wb1[...] = w1_ref[...].astype(BF16)
        wb2[...] = w2_ref[...].astype(BF16)

    o0 = jnp.dot(y_ref[...], wb0[...], preferred_element_type=F32)
    o1 = jnp.dot(v_ref[...], wb1[...], preferred_element_type=F32)
    o2 = jnp.dot(u_ref[...], wb2[...], preferred_element_type=F32)
    merged = (g0_ref[...].astype(F32) * o0 + g1_ref[...].astype(F32) * o1
              + g2_ref[...].astype(F32) * o2)
    o_ref[...] = merged.astype(BF16)


def _merge(y, v, u, gates, ssd_out, sc_out, cf_out, l):
    t, d = y.shape
    kv, ku = v.shape[1], u.shape[1]
    tm, tn = 512, 256
    nj = d // tn
    one = pl.Buffered(1)

    def wspec(k):
        return pl.BlockSpec((None, k, tn), lambda j, i: (l, 0, j), pipeline_mode=one)

    return pl.pallas_call(
        _merge_kernel,
        grid=(nj, t // tm),
        in_specs=[
            pl.BlockSpec((tm, d), lambda j, i: (i, 0)),
            pl.BlockSpec((tm, kv), lambda j, i: (i, 0)),
            pl.BlockSpec((tm, ku), lambda j, i: (i, 0)),
            pl.BlockSpec((tm, tn), lambda j, i: (i, j)),
            pl.BlockSpec((tm, tn), lambda j, i: (i, nj + j)),
            pl.BlockSpec((tm, tn), lambda j, i: (i, 2 * nj + j)),
            wspec(d), wspec(kv), wspec(ku),
        ],
        out_specs=pl.BlockSpec((tm, tn), lambda j, i: (i, j)),
        out_shape=jax.ShapeDtypeStruct((t, d), BF16),
        scratch_shapes=[pltpu.VMEM((d, tn), BF16), pltpu.VMEM((kv, tn), BF16),
                        pltpu.VMEM((ku, tn), BF16)],
        compiler_params=_cparams(2, 48),
    )(y, v, u, gates, gates, gates, ssd_out, sc_out, cf_out)


def _wo_kernel(m_ref, w_ref, x_ref, g_ref, o_ref, wb, *, tm, geo):
    i = pl.program_id(1)

    @pl.when(i == 0)
    def _():
        wb[...] = w_ref[...].astype(BF16)

    row = _mod_row(i * tm, *geo)
    out = jnp.dot(m_ref[...], wb[...], preferred_element_type=F32)
    o_ref[...] = x_ref[...] + g_ref[pl.ds(row, 1), :] * out


def _wo(merged, w_o, x, mod, l, geo):
    t, d = x.shape
    r = mod.shape[1]
    tm, tn = 512, 512
    nj = d // tn
    return pl.pallas_call(
        functools.partial(_wo_kernel, tm=tm, geo=geo),
        grid=(nj, t // tm),
        in_specs=[
            pl.BlockSpec((tm, d), lambda j, i: (i, 0)),
            pl.BlockSpec((None, d, tn), lambda j, i: (l, 0, j)),
            pl.BlockSpec((tm, tn), lambda j, i: (i, j)),
            pl.BlockSpec((None, r, tn), lambda j, i: (l, 0, 2 * nj + j)),
        ],
        out_specs=pl.BlockSpec((tm, tn), lambda j, i: (i, j)),
        out_shape=jax.ShapeDtypeStruct((t, d), F32),
        scratch_shapes=[pltpu.VMEM((d, tn), BF16)],
        compiler_params=_cparams(2, 48),
    )(merged, w_o, x, mod)


def _router_kernel(x_ref, g_ref, sh_ref, s_ref, rw_ref, rb_ref,
                   hp_ref, idx_ref, wt_ref, rank_ref, cnt_ref, carry, *, tm, geo, n_exp):
    i = pl.program_id(0)

    @pl.when(i == 0)
    def _():
        carry[...] = jnp.zeros(carry.shape, F32)

    row = _mod_row(i * tm, *geo)
    x = x_ref[...]
    y = x * lax.rsqrt(jnp.mean(x * x, axis=-1, keepdims=True) + EPS) * g_ref[...]
    h = y * (1.0 + s_ref[pl.ds(row, 1), :]) + sh_ref[pl.ds(row, 1), :]
    half = h.shape[1] // 2
    lo = lax.bitcast_convert_type(h[:, :half].astype(BF16).astype(F32), U32)
    hi = lax.bitcast_convert_type(h[:, half:].astype(BF16).astype(F32), U32)
    hp_ref[...] = (lo >> 16) | (hi & jnp.uint32(0xFFFF0000))

    logits = jnp.dot(h, rw_ref[...], precision=HIGHEST, preferred_element_type=F32)
    lt = logits.T[0:n_exp, :]
    score = _sigmoid(lt)
    sel = score + rb_ref[...]
    srow = [sel[e:e + 1, :] for e in range(n_exp)]
    prow = [score[e:e + 1, :] for e in range(n_exp)]

    def group_score(gi):
        v = srow[gi * GROUP_SIZE:(gi + 1) * GROUP_SIZE]
        best = None
        for a in range(GROUP_SIZE):
            for b in range(a + 1, GROUP_SIZE):
                s = v[a] + v[b]
                best = s if best is None else jnp.maximum(best, s)
        return best

    best_g = group_score(0)
    bi = jnp.zeros(best_g.shape, I32)
    for gi in range(1, N_EXPERT_GROUPS):
        gs = group_score(gi)
        upd = gs > best_g
        best_g = jnp.where(upd, gs, best_g)
        bi = jnp.where(upd, gi, bi)

    def pick(rows, j):
        out = rows[j]
        for gi in range(1, N_EXPERT_GROUPS):
            out = jnp.where(bi == gi, rows[gi * GROUP_SIZE + j], out)
        return out

    v = [pick(srow, j) for j in range(GROUP_SIZE)]
    p = [pick(prow, j) for j in range(GROUP_SIZE)]

    def argmax_first(vals):
        best, arg = vals[0], jnp.zeros(vals[0].shape, I32)
        for j in range(1, GROUP_SIZE):
            upd = vals[j] > best
            best = jnp.where(upd, vals[j], best)
            arg = jnp.where(upd, j, arg)
        return arg

    i1 = argmax_first(v)
    i2 = argmax_first([jnp.where(i1 == j, -jnp.inf, v[j]) for j in range(GROUP_SIZE)])
    p1 = sum(jnp.where(i1 == j, p[j], 0.0) for j in range(GROUP_SIZE))
    p2 = sum(jnp.where(i2 == j, p[j], 0.0) for j in range(GROUP_SIZE))
    den = p1 + p2
    e1 = bi * GROUP_SIZE + i1
    e2 = bi * GROUP_SIZE + i2
    idx_ref[0:1, :] = e1
    idx_ref[1:2, :] = e2
    wt_ref[0:1, :] = p1 / den
    wt_ref[1:2, :] = p2 / den

    erow = lax.broadcasted_iota(I32, (n_exp, tm), 0)
    oh1 = (erow == e1).astype(F32)
    oh2 = (erow == e2).astype(F32)
    cnt = oh1 + oh2
    before = (lax.broadcasted_iota(I32, (tm, tm), 0)
              < lax.broadcasted_iota(I32, (tm, tm), 1)).astype(BF16)
    prior = jnp.dot(cnt.astype(BF16), before, preferred_element_type=F32) + carry[...]
    rank_ref[0:1, :] = jnp.sum(oh1 * prior, axis=0, keepdims=True).astype(I32)
    rank_ref[1:2, :] = jnp.sum(oh2 * prior, axis=0, keepdims=True).astype(I32)
    carry[...] = carry[...] + jnp.sum(cnt, axis=1, keepdims=True)
    cnt_ref[...] = jnp.broadcast_to(carry[...], cnt_ref.shape)


def _router(x, g, mod, l, rw_pad, rb_col, geo):
    t, d = x.shape
    r = mod.shape[1]
    n_exp = rb_col.shape[0]
    tm = 256
    return pl.pallas_call(
        functools.partial(_router_kernel, tm=tm, geo=geo, n_exp=n_exp),
        grid=(t // tm,),
        in_specs=[
            pl.BlockSpec((tm, d), lambda i: (i, 0)),
            pl.BlockSpec((1, d), lambda i: (0, 0)),
            pl.BlockSpec((None, r, d), lambda i: (l, 0, 3)),
            pl.BlockSpec((None, r, d), lambda i: (l, 0, 4)),
            pl.BlockSpec(rw_pad.shape, lambda i: (0, 0)),
            pl.BlockSpec(rb_col.shape, lambda i: (0, 0)),
        ],
        out_specs=[
            pl.BlockSpec((tm, d // 2), lambda i: (i, 0)),
            pl.BlockSpec((2, tm), lambda i: (0, i)),
            pl.BlockSpec((2, tm), lambda i: (0, i)),
            pl.BlockSpec((2, tm), lambda i: (0, i)),
            pl.BlockSpec((n_exp, 128), lambda i: (0, 0)),
        ],
        out_shape=[
            jax.ShapeDtypeStruct((t, d // 2), U32),
            jax.ShapeDtypeStruct((2, t), I32),
            jax.ShapeDtypeStruct((2, t), F32),
            jax.ShapeDtypeStruct((2, t), I32),
            jax.ShapeDtypeStruct((n_exp, 128), F32),
        ],
        scratch_shapes=[pltpu.VMEM((n_exp, 1), F32)],
        compiler_params=_cparams(1, 32),
    )(x, g, mod, mod, rw_pad, rb_col)


def _row_copy(src, dst, sem):
    return pltpu.make_async_copy(src, dst, sem)


def _dispatch_kernel(idx_s, rank_s, off_s, h_ref, xs_in, xs_out, sem, *, tm, t):
    del xs_in
    base = pl.program_id(0) * tm

    def issue(r, carry):
        for k in range(2):
            pos = k * t + base + r
            dst = off_s[idx_s[pos]] + rank_s[pos]
            _row_copy(h_ref.at[pl.ds(r, 1)], xs_out.at[pl.ds(dst, 1)], sem).start()
        return carry

    lax.fori_loop(0, tm, issue, 0)

    def drain(r, carry):
        for k in range(2):
            _row_copy(h_ref.at[pl.ds(0, 1)], xs_out.at[pl.ds(0, 1)], sem).wait()
        return carry

    lax.fori_loop(0, tm, drain, 0)


def _dispatch(hp, idx_flat, rank_flat, off, n_rows):
    t, w = hp.shape
    tm = 256
    xs0 = jnp.zeros((n_rows, w), U32)
    return pl.pallas_call(
        functools.partial(_dispatch_kernel, tm=tm, t=t),
        grid_spec=pltpu.PrefetchScalarGridSpec(
            num_scalar_prefetch=3,
            grid=(t // tm,),
            in_specs=[pl.BlockSpec((tm, w), lambda i, *_: (i, 0)),
                      pl.BlockSpec(memory_space=pl.ANY)],
            out_specs=pl.BlockSpec(memory_space=pl.ANY),
            scratch_shapes=[pltpu.SemaphoreType.DMA(())],
        ),
        out_shape=jax.ShapeDtypeStruct((n_rows, w), U32),
        input_output_aliases={4: 0},
        compiler_params=_cparams(1, 32),
    )(idx_flat, rank_flat, off, hp, xs0)


def _up_kernel(te_s, nu_s, x_ref, wg_ref, wu_ref, o_ref):
    i = pl.program_id(1)

    @pl.when(i < nu_s[0])
    def _():
        pk = x_ref[...]
        lo = lax.bitcast_convert_type(pk << 16, F32)
        hi = lax.bitcast_convert_type(pk & jnp.uint32(0xFFFF0000), F32)
        x = jnp.concatenate([lo, hi], axis=1).astype(BF16)
        gt = jnp.dot(x, wg_ref[...].astype(BF16), preferred_element_type=F32)
        up = jnp.dot(x, wu_ref[...].astype(BF16), preferred_element_type=F32)
        o_ref[...] = (_silu(gt) * up).astype(BF16)

    @pl.when(i >= nu_s[0])
    def _():
        o_ref[...] = jnp.zeros(o_ref.shape, BF16)


def _moe_up(xs, te, nu, w_gate, w_up, l, tm):
    n_rows, w = xs.shape
    d, f = w_gate.shape[2], w_gate.shape[3]
    tf = 256
    wspec = pl.BlockSpec((None, None, d, tf), lambda j, i, te_s, nu_s: (l, te_s[i], 0, j))
    return pl.pallas_call(
        _up_kernel,
        grid_spec=pltpu.PrefetchScalarGridSpec(
            num_scalar_prefetch=2,
            grid=(f // tf, n_rows // tm),
            in_specs=[pl.BlockSpec((tm, w), lambda j, i, *_: (i, 0)), wspec, wspec],
            out_specs=pl.BlockSpec((tm, tf), lambda j, i, *_: (i, j)),
        ),
        out_shape=jax.ShapeDtypeStruct((n_rows, f), BF16),
        compiler_params=_cparams(2, 48),
    )(te, nu, xs, w_gate, w_up)


def _down_kernel(te_s, nu_s, a_ref, w_ref, o_ref):
    i = pl.program_id(1)

    @pl.when(i < nu_s[0])
    def _():
        o_ref[...] = jnp.dot(a_ref[...], w_ref[...].astype(BF16), preferred_element_type=F32)

    @pl.when(i >= nu_s[0])
    def _():
        o_ref[...] = jnp.zeros(o_ref.shape, F32)


def _moe_down(act, te, nu, w_down, l, tm):
    n_rows, f = act.shape
    d = w_down.shape[3]
    tn = 512
    return pl.pallas_call(
        _down_kernel,
        grid_spec=pltpu.PrefetchScalarGridSpec(
            num_scalar_prefetch=2,
            grid=(d // tn, n_rows // tm),
            in_specs=[pl.BlockSpec((tm, f), lambda j, i, *_: (i, 0)),
                      pl.BlockSpec((None, None, f, tn), lambda j, i, te_s, nu_s: (l, te_s[i], 0, j))],
            out_specs=pl.BlockSpec((tm, tn), lambda j, i, *_: (i, j)),
        ),
        out_shape=jax.ShapeDtypeStruct((n_rows, d), F32),
        compiler_params=_cparams(2, 32),
    )(te, nu, act, w_down)


def _combine_kernel(idx_s, rank_s, off_s, x_ref, wt_ref, g2_ref, ng_ref, sh_ref, s_ref, ys_ref,
                    *rest, tm, t, geo, final):
    if final:
        o_ref, buf, sem = rest
    else:
        o_ref, h_ref, buf, sem = rest
    base = pl.program_id(0) * tm

    def issue(r, carry):
        for k in range(2):
            pos = k * t + base + r
            src = off_s[idx_s[pos]] + rank_s[pos]
            _row_copy(ys_ref.at[pl.ds(src, 1)], buf.at[k, pl.ds(r, 1)], sem).start()
        return carry

    lax.fori_loop(0, tm, issue, 0)

    def drain(r, carry):
        for k in range(2):
            _row_copy(ys_ref.at[pl.ds(0, 1)], buf.at[k, pl.ds(0, 1)], sem).wait()
        return carry

    lax.fori_loop(0, tm, drain, 0)

    row = _mod_row(base, *geo)
    wt = wt_ref[...]
    moe = wt[:, 0:1] * buf[0] + wt[:, 1:2] * buf[1]
    xn = x_ref[...] + g2_ref[pl.ds(row, 1), :] * moe
    y = xn * lax.rsqrt(jnp.mean(xn * xn, axis=-1, keepdims=True) + EPS) * ng_ref[...]
    if final:
        o_ref[...] = y
    else:
        o_ref[...] = xn
        h_ref[...] = (y * (1.0 + s_ref[pl.ds(row, 1), :]) + sh_ref[pl.ds(row, 1), :]).astype(BF16)


def _combine(x, wt_col, mod, l, norm_g, ys, idx_flat, rank_flat, off, geo, final):
    t, d = x.shape
    r = mod.shape[1]
    tm = 256
    l_next = l if final else l + 1
    out_specs = [pl.BlockSpec((tm, d), lambda i, *_: (i, 0))]
    out_shape = [jax.ShapeDtypeStruct((t, d), F32)]
    if not final:
        out_specs.append(pl.BlockSpec((tm, d), lambda i, *_: (i, 0)))
        out_shape.append(jax.ShapeDtypeStruct((t, d), BF16))
    return pl.pallas_call(
        functools.partial(_combine_kernel, tm=tm, t=t, geo=geo, final=final),
        grid_spec=pltpu.PrefetchScalarGridSpec(
            num_scalar_prefetch=3,
            grid=(t // tm,),
            in_specs=[
                pl.BlockSpec((tm, d), lambda i, *_: (i, 0)),
                pl.BlockSpec((tm, 2), lambda i, *_: (i, 0)),
                pl.BlockSpec((None, r, d), lambda i, *_: (l, 0, 5)),
                pl.BlockSpec((1, d), lambda i, *_: (0, 0)),
                pl.BlockSpec((None, r, d), lambda i, *_: (l_next, 0, 0)),
                pl.BlockSpec((None, r, d), lambda i, *_: (l_next, 0, 1)),
                pl.BlockSpec(memory_space=pl.ANY),
            ],
            out_specs=out_specs,
            scratch_shapes=[pltpu.VMEM((2, tm, d), F32), pltpu.SemaphoreType.DMA(())],
        ),
        out_shape=out_shape,
        compiler_params=_cparams(1, 40),
    )(idx_flat, rank_flat, off, x, wt_col, mod, norm_g, mod, mod, ys)


def kernel(x_prompt, x_sample, state_ssd, c, c_ctx, ada_w, ada_b, norm1_g, norm2_g, w_in,
           ssd_conv_w, ssd_conv_b, ssd_dt_bias, ssd_a_log, ssd_d, ssd_norm_g, ssd_out,
           sc_conv_w, sc_conv_b, sc_out, cf_conv_w, cf_conv_b, cf_ln_g, cf_ln_b, cf_out, w_o,
           router_w, router_bias, moe_w_gate, moe_w_up, moe_w_down, final_g):
    n_ctx, seq, d = x_prompt.shape
    n_dec, dec_seq, _ = x_sample.shape
    depth = w_in.shape[0]
    n_heads = d // HEAD_DIM
    gn = SSD_GROUPS * SSD_STATE
    xbc_w = d + 2 * gn
    sc_w = sc_out.shape[1]
    cf_w = cf_out.shape[1]
    n_exp = router_w.shape[1]
    t_ctx = n_ctx * seq
    t = t_ctx + n_dec * dec_seq
    geo = (t_ctx, dec_seq, n_dec)
    off_xbc = d
    off_dt = off_xbc + xbc_w
    off_sc = off_dt + 2 * n_heads
    off_cf = off_sc + 3 * sc_w
    off_gate = off_cf + 2 * cf_w

    x = jnp.concatenate([x_prompt.reshape(t_ctx, d), x_sample.reshape(n_dec * dec_seq, d)], axis=0)
    n_mod_rows = -(-(n_dec + 1) // 8) * 8
    cond = jnp.concatenate([c, c_ctx[None, :],
                            jnp.zeros((n_mod_rows - n_dec - 1, d), F32)], axis=0)
    mod = _ada_table(cond, ada_w, ada_b)

    h0_lat = state_ssd.reshape(n_dec, depth, 2, n_heads // 2, PAIR, SSD_STATE)
    state_buf = jnp.zeros((n_ctx, depth, 2, n_heads // 2, PAIR, SSD_STATE), F32)
    rw_pad = jnp.pad(router_w, ((0, 0), (0, 128 - n_exp)))
    rb_col = router_bias.reshape(n_exp, 1)
    tm_e = 512
    n_rows = 2 * t + n_exp * tm_e
    n_tiles = n_rows // tm_e

    h = _normmod(x, norm1_g[0:1], mod, 0, geo)
    y_final = None
    for l in range(depth):
        z = _proj(h, w_in, l, [0], d, 512, 1024, _epi_identity, [], [], F32)
        xbc = _proj(
            h, w_in, l, [off_xbc], xbc_w, 512, dec_seq,
            functools.partial(_epi_xbc, tm=dec_seq, t_ctx=t_ctx, seq=seq, dec_seq=dec_seq),
            [ssd_conv_w, ssd_conv_b.reshape(depth, 1, xbc_w)],
            [pl.BlockSpec((None, 3, 512), lambda j, i: (l, 0, j)),
             pl.BlockSpec((None, 1, 512), lambda j, i: (l, 0, j))], F32)
        dt = _proj(h, w_in, l, [off_dt], 2 * n_heads, 2 * n_heads, 1024, _epi_dt,
                   [ssd_dt_bias.reshape(depth, 1, 2 * n_heads)],
                   [pl.BlockSpec((None, 1, 2 * n_heads), lambda j, i: (l, 0, 0))], F32)
        v = _proj(
            h, w_in, l, [off_sc, off_sc + sc_w, off_sc + 2 * sc_w], sc_w, 256, 512,
            functools.partial(_epi_sc, tm=512, t_ctx=t_ctx, seq=seq),
            [sc_conv_w, sc_conv_b.reshape(depth, 1, sc_w)],
            [pl.BlockSpec((None, 3, 256), lambda j, i: (l, 0, j)),
             pl.BlockSpec((None, 1, 256), lambda j, i: (l, 0, j))], BF16)
        u_cf = _proj(h, w_in, l, [off_cf, off_cf + cf_w], cf_w, 256, 1024, _epi_cf, [], [], F32)
        gates = _proj(h, w_in, l, [off_gate], 3 * d, 512, 1024, _epi_sigmoid, [], [], BF16)

        alog = ssd_a_log[l].reshape(1, 2 * n_heads)
        dskip = jnp.repeat(ssd_d[l], HEAD_DIM).reshape(1, d)
        ng = ssd_norm_g[l:l + 1]
        scan = dict(d=d, l=l, shape=(n_ctx, seq, n_dec, dec_seq))
        yf, state_buf = _ssd_pass(xbc, dt, alog, h0_lat, state_buf, reverse=False, **scan)
        y_ssd, state_buf = _ssd_pass(xbc, dt, alog, h0_lat, state_buf, reverse=True,
                                     fused_in=(yf, z, dskip, ng), **scan)

        u = _cfconv(u_cf, cf_conv_w[l], cf_conv_b[l:l + 1], cf_ln_g[l:l + 1], cf_ln_b[l:l + 1],
                    t_ctx=t_ctx, seq=seq, dec_seq=dec_seq)

        merged = _merge(y_ssd, v, u, gates, ssd_out, sc_out, cf_out, l)
        x = _wo(merged, w_o, x, mod, l, geo)

        hp, idx, wt, rank, cnt = _router(x, norm2_g[l:l + 1], mod, l, rw_pad, rb_col, geo)
        counts = cnt[:, 0].astype(I32)
        tiles = (counts + tm_e - 1) // tm_e
        tile_end = jnp.cumsum(tiles)
        off = (tile_end - tiles) * tm_e
        n_used = tile_end[-1]
        tile_id = jnp.minimum(jnp.arange(n_tiles, dtype=I32), n_used - 1)
        te = jnp.sum(tile_id[:, None] >= tile_end[None, :], axis=1).astype(I32)
        nu = n_used.reshape(1).astype(I32)
        idx_flat = idx.reshape(2 * t)
        rank_flat = rank.reshape(2 * t)
        xs = _dispatch(hp, idx_flat, rank_flat, off, n_rows)
        act = _moe_up(xs, te, nu, moe_w_gate, moe_w_up, l, tm_e)
        ys = _moe_down(act, te, nu, moe_w_down, l, tm_e)
        final = l == depth - 1
        ng_next = final_g.reshape(1, d) if final else norm1_g[l + 1:l + 2]
        outs = _combine(x, wt.T, mod, l, ng_next, ys, idx_flat, rank_flat, off, geo, final)
        if final:
            (y_final,) = outs
        else:
            x, h = outs

    y_prompt = y_final[:t_ctx].reshape(n_ctx, seq, d)
    y_sample = y_final[t_ctx:].reshape(n_dec, dec_seq, d)
    new_state = state_buf.reshape(n_ctx, depth, 2, n_heads, HEAD_DIM, SSD_STATE)
    return (y_prompt, y_sample, new_state)
```

```python
import functools

import jax
import jax.numpy as jnp
import numpy as np
from jax import lax
from jax.experimental import pallas as pl
from jax.experimental.pallas import tpu as pltpu

F32 = jnp.float32
BF16 = jnp.bfloat16
I32 = jnp.int32
U32 = jnp.uint32
HIGHEST = lax.Precision.HIGHEST

EPS = 1e-6
GRID_W = 64
HEAD_DIM = 64
PAIR = 2 * HEAD_DIM
SSD_GROUPS = 8
SSD_STATE = 128
CHUNK = 128
N_EXPERT_GROUPS = 4
GROUP_SIZE = 4
MIB = 1024 * 1024
VMEM_BUDGET_MIB = 56


def _cparams(n_axes, vmem_mib):
    return pltpu.CompilerParams(
        dimension_semantics=("arbitrary",) * n_axes,
        vmem_limit_bytes=min(vmem_mib, VMEM_BUDGET_MIB) * MIB)


def _sigmoid(x):
    return 1.0 / (1.0 + jnp.exp(-x))


def _silu(x):
    return x * _sigmoid(x)


def _mod_row(start, t_ctx, dec_seq, n_dec):
    return jnp.where(start < t_ctx, n_dec, lax.div(start - t_ctx, dec_seq))


def _ada_kernel(c_ref, w_ref, b_ref, o_ref):
    s = _silu(c_ref[...])
    o_ref[...] = jnp.dot(s, w_ref[...], precision=HIGHEST,
                         preferred_element_type=F32) + b_ref[...]


def _ada_table(cond, ada_w, ada_b):
    depth, d, n = ada_w.shape
    r = cond.shape[0]
    tn = 512
    return pl.pallas_call(
        _ada_kernel,
        grid=(depth, n // tn),
        in_specs=[
            pl.BlockSpec((r, d), lambda l, j: (0, 0)),
            pl.BlockSpec((None, d, tn), lambda l, j: (l, 0, j)),
            pl.BlockSpec((None, 1, tn), lambda l, j: (l, 0, j)),
        ],
        out_specs=pl.BlockSpec((None, r, tn), lambda l, j: (l, 0, j)),
        out_shape=jax.ShapeDtypeStruct((depth, r, n), F32),
        compiler_params=_cparams(2, 32),
    )(cond, ada_w, ada_b.reshape(depth, 1, n))


def _split_specs(x_parts, tm, width, n_ctx_tiles, tile_axis, col_axis=None):
    xa, xb = x_parts
    b_off = n_ctx_tiles if xa is xb else 0

    def col(ids):
        return 0 if col_axis is None else ids[col_axis]

    def amap(*ids):
        return (jnp.minimum(ids[tile_axis], n_ctx_tiles - 1), col(ids))

    def bmap(*ids):
        return (b_off + jnp.maximum(ids[tile_axis] - n_ctx_tiles, 0), col(ids))

    return [pl.BlockSpec((tm, width), amap), pl.BlockSpec((tm, width), bmap)]


def _normmod_kernel(xa_ref, xb_ref, g_ref, sh_ref, s_ref, o_ref, *, tm, geo):
    start = pl.program_id(0) * tm
    row = _mod_row(start, *geo)

    def emit(x_ref):
        x = x_ref[...]
        y = x * lax.rsqrt(jnp.mean(x * x, axis=-1, keepdims=True) + EPS) * g_ref[...]
        o_ref[...] = (y * (1.0 + s_ref[pl.ds(row, 1), :]) + sh_ref[pl.ds(row, 1), :]).astype(BF16)

    pl.when(start < geo[0])(lambda: emit(xa_ref))
    pl.when(start >= geo[0])(lambda: emit(xb_ref))


def _normmod(x_parts, g, mod, l, geo):
    d = x_parts[0].shape[1]
    t = geo[0] + geo[1] * geo[2]
    r = mod.shape[1]
    tm = 256
    return pl.pallas_call(
        functools.partial(_normmod_kernel, tm=tm, geo=geo),
        grid=(t // tm,),
        in_specs=_split_specs(x_parts, tm, d, geo[0] // tm, 0) + [
            pl.BlockSpec((1, d), lambda i: (0, 0)),
            pl.BlockSpec((None, r, d), lambda i: (l, 0, 0)),
            pl.BlockSpec((None, r, d), lambda i: (l, 0, 1)),
        ],
        out_specs=pl.BlockSpec((tm, d), lambda i: (i, 0)),
        out_shape=jax.ShapeDtypeStruct((t, d), BF16),
        compiler_params=_cparams(1, 32),
    )(*x_parts, g, mod, mod)


def _proj_kernel(*refs, n_dots, n_extra, epilogue):
    h_ref = refs[0]
    w_refs = refs[1:1 + n_dots]
    extra = refs[1 + n_dots:1 + n_dots + n_extra]
    o_ref = refs[1 + n_dots + n_extra]
    wb = refs[2 + n_dots + n_extra:]
    i = pl.program_id(1)

    @pl.when(i == 0)
    def _():
        for d in range(n_dots):
            wb[d][...] = w_refs[d][...].astype(BF16)

    h = h_ref[...]
    accs = [jnp.dot(h, wb[d][...], preferred_element_type=F32) for d in range(n_dots)]
    o_ref[...] = epilogue(accs, extra, i).astype(o_ref.dtype)


def _proj(h, w_in, l, col_offs, n_cols, tn, tm, epilogue, extra, extra_specs, out_dtype):
    t, k = h.shape
    n_dots = len(col_offs)

    def wspec(off):
        return pl.BlockSpec((pl.Squeezed(), pl.Element(k), pl.Element(tn)),
                            lambda j, i: (l, 0, pl.multiple_of(off + j * tn, 128)))

    out_bytes = jnp.dtype(out_dtype).itemsize
    vmem = (2 * tm * k * 2 + n_dots * k * tn * (2 * 4 + 2) + 2 * tm * tn * out_bytes) // MIB + 8
    return pl.pallas_call(
        functools.partial(_proj_kernel, n_dots=n_dots, n_extra=len(extra), epilogue=epilogue),
        grid=(n_cols // tn, t // tm),
        in_specs=[pl.BlockSpec((tm, k), lambda j, i: (i, 0))]
        + [wspec(off) for off in col_offs] + list(extra_specs),
        out_specs=pl.BlockSpec((tm, tn), lambda j, i: (i, j)),
        out_shape=jax.ShapeDtypeStruct((t, n_cols), out_dtype),
        scratch_shapes=[pltpu.VMEM((k, tn), BF16)] * n_dots,
        compiler_params=_cparams(2, vmem),
    )(h, *([w_in] * n_dots), *extra)


def _conv3_rows(u, w_ref, b_ref, period):
    rows = u.shape[0]
    pos = lax.broadcasted_iota(I32, u.shape, 0) & (period - 1)
    prev = jnp.where(pos == 0, 0.0, pltpu.roll(u, 1, 0))
    nxt = jnp.where(pos == period - 1, 0.0, pltpu.roll(u, rows - 1, 0))
    return w_ref[0:1, :] * prev + w_ref[1:2, :] * u + w_ref[2:3, :] * nxt + b_ref[...]


def _epi_identity(accs, extra, i):
    return accs[0]


def _epi_sigmoid(accs, extra, i):
    return _sigmoid(accs[0])


def _epi_xbc(accs, extra, i, *, tm, t_ctx, seq, dec_seq):
    period = jnp.where(i * tm < t_ctx, seq, dec_seq)
    return _silu(_conv3_rows(accs[0], extra[0], extra[1], period))


def _epi_dt(accs, extra, i):
    x = accs[0] + extra[0][...]
    return jnp.maximum(x, 0.0) + jnp.log1p(jnp.exp(-jnp.abs(x)))


def _epi_sc(accs, extra, i, *, tm, t_ctx, seq):
    bg, cg, hv = accs
    period = jnp.where(i * tm < t_ctx, seq, GRID_W)
    return bg * _conv3_rows(cg * hv, extra[0], extra[1], period)


def _epi_cf(accs, extra, i):
    return accs[0] * _sigmoid(accs[1])


TBL_BLOCK, TBL_FIRST, TBL_LAST, TBL_LATENT, TBL_H0, TBL_STATE, TBL_FIELDS = 0, 1, 2, 3, 4, 5, 6


def _ssd_kernel(tbl, *refs, reverse, fused, n_heads, n_steps):
    it = iter(refs)
    xs_ref, b_ref, c_ref, dt_ref, alog_ref, expand_ref, h0_ref = (next(it) for _ in range(7))
    if fused:
        yf_ref, z_ref, dskip_ref, ng_ref = (next(it) for _ in range(4))
    next(it)
    y_ref, st_ref, h_scr = next(it), next(it), next(it)
    ybuf = next(it) if fused else None

    step = pl.program_id(0)
    first = tbl[TBL_FIRST * n_steps + step] == 1
    latent = tbl[TBL_LATENT * n_steps + step] == 1

    @pl.when(first & latent)
    def _():
        h_scr[...] = h0_ref[...]

    @pl.when(first & jnp.logical_not(latent))
    def _():
        h_scr[...] = jnp.zeros(h_scr.shape, F32)

    q = CHUNK
    dt = dt_ref[...]
    dta = dt * (-jnp.exp(alog_ref[...]))
    r_i = lax.broadcasted_iota(I32, (q, q), 0)
    c_i = lax.broadcasted_iota(I32, (q, q), 1)
    tri = (c_i >= r_i) if reverse else (c_i <= r_i)
    acum = jnp.dot(tri.astype(F32), dta, precision=HIGHEST, preferred_element_type=F32)
    acum_t = acum.T
    last = acum[0:1, :] if reverse else acum[q - 1:q, :]
    expa = jnp.exp(acum)
    dend = jnp.exp(last - acum)
    cdec = jnp.exp(last)
    lane_lo = lax.broadcasted_iota(I32, (q, PAIR), 1) < HEAD_DIM
    row_lo = lax.broadcasted_iota(I32, (PAIR, SSD_STATE), 0) < HEAD_DIM
    d0 = n_heads if reverse else 0
    hg = n_heads // SSD_GROUPS
    nt = (((1,), (1,)), ((), ()))

    per_head = jnp.concatenate([dt, dt * dend, expa], axis=0)
    hi = per_head.astype(BF16)
    lo = (per_head - hi.astype(F32)).astype(BF16)
    on_lanes = jnp.dot(jnp.concatenate([hi, lo], axis=1), expand_ref[...],
                       preferred_element_type=F32)
    xs_all = xs_ref[...]
    xdt_all = xs_all * on_lanes[0:q]
    xw_all = xs_all * on_lanes[q:2 * q]
    expa_all = on_lanes[2 * q:3 * q]

    pairs_g = hg // 2
    for g in range(SSD_GROUPS):
        cg = c_ref[:, g * SSD_STATE:(g + 1) * SSD_STATE].astype(BF16)
        bg = b_ref[:, g * SSD_STATE:(g + 1) * SSD_STATE].astype(BF16)
        cb = lax.dot_general(cg, bg, nt, preferred_element_type=F32)
        p0 = g * pairs_g
        gslab = slice(p0 * PAIR, (p0 + pairs_g) * PAIR)
        hgrp = jnp.concatenate([h_scr[p0 + jp] for jp in range(pairs_g)], axis=0)
        y_in = lax.dot_general(cg, hgrp.astype(BF16), nt,
                               preferred_element_type=F32) * expa_all[:, gslab]
        st = jnp.dot(xw_all[:, gslab].T.astype(BF16), bg, preferred_element_type=F32)
        for jp in range(pairs_g):
            pair = p0 + jp
            hx = d0 + 2 * pair
            rows = slice(jp * PAIR, (jp + 1) * PAIR)
            keep = jnp.where(row_lo, cdec[:, hx:hx + 1], cdec[:, hx + 1:hx + 2])
            h_scr[pair] = keep * hgrp[rows] + st[rows]

            def decay_w(hh):
                diff = acum[:, hh:hh + 1] - acum_t[hh:hh + 1, :]
                return (cb * jnp.exp(jnp.where(tri, diff, -jnp.inf))).astype(BF16)

            slab = slice(pair * PAIR, (pair + 1) * PAIR)
            w = jnp.concatenate([decay_w(hx), decay_w(hx + 1)], axis=1)
            xdt = xdt_all[:, slab]
            xbd = jnp.concatenate([jnp.where(lane_lo, xdt, 0.0),
                                   jnp.where(lane_lo, 0.0, xdt)], axis=0).astype(BF16)
            y = jnp.dot(w, xbd, preferred_element_type=F32) + y_in[:, rows]
            if fused:
                ybuf[:, slab] = y
            else:
                y_ref[:, slab] = y

    @pl.when((tbl[TBL_LAST * n_steps + step] == 1) & jnp.logical_not(latent))
    def _():
        st_ref[...] = h_scr[...]

    if fused:
        yt = ybuf[...] + yf_ref[...] + dskip_ref[...] * xs_all
        yt = yt * _silu(z_ref[...])
        yn = yt * lax.rsqrt(jnp.mean(yt * yt, axis=-1, keepdims=True) + EPS) * ng_ref[...]
        y_ref[...] = yn.astype(BF16)


def _ssd_table(n_ctx, seq, n_dec, dec_seq, reverse):
    steps = []
    blk = 0
    for n_seq, length, latent in ((n_ctx, seq, 0), (n_dec, dec_seq, 1)):
        n_chunks = length // CHUNK
        for b in range(n_seq):
            for c in range(n_chunks):
                steps.append((blk + c, int(c == 0), int(c == n_chunks - 1), latent,
                              b if latent else 0, n_ctx - 1 if latent else b))
            blk += n_chunks
    if reverse:
        steps = [(s[0], s[2], s[1], s[3], s[4], s[5]) for s in reversed(steps)]
    return np.asarray(steps, np.int32).T.reshape(-1), len(steps)


def _ssd_pass(xbc, dt, alog, h0, state_buf, *, d, l, shape, reverse, fused_in=None):
    n_heads = d // HEAD_DIM
    gn = SSD_GROUPS * SSD_STATE
    direction = 1 if reverse else 0
    fused = fused_in is not None
    t = xbc.shape[0]
    table, n_steps = _ssd_table(*shape, reverse)

    def field(f):
        return lambda s, tbl: tbl[f * n_steps + s]

    blk = field(TBL_BLOCK)
    row_spec = pl.BlockSpec((CHUNK, d), lambda s, tbl: (blk(s, tbl), 0))
    vec_spec = pl.BlockSpec((1, d), lambda s, tbl: (0, 0))
    state_block = (None, None, None, n_heads // 2, PAIR, SSD_STATE)
    in_specs = [
        row_spec,
        pl.BlockSpec((CHUNK, gn), lambda s, tbl: (blk(s, tbl), d // gn)),
        pl.BlockSpec((CHUNK, gn), lambda s, tbl: (blk(s, tbl), d // gn + 1)),
        pl.BlockSpec((CHUNK, 2 * n_heads), lambda s, tbl: (blk(s, tbl), 0)),
        pl.BlockSpec((1, 2 * n_heads), lambda s, tbl: (0, 0)),
        pl.BlockSpec((4 * n_heads, d), lambda s, tbl: (0, 0)),
        pl.BlockSpec(state_block,
                     lambda s, tbl: (field(TBL_H0)(s, tbl), l, direction, 0, 0, 0)),
    ]
    expand = np.zeros((2 * n_heads, d), np.float32)
    expand[direction * n_heads + np.arange(d) // HEAD_DIM, np.arange(d)] = 1.0
    expand = np.concatenate([expand, expand], axis=0)
    args = [jnp.asarray(table), xbc, xbc, xbc, dt, alog, jnp.asarray(expand, BF16), h0]
    if fused:
        in_specs += [row_spec, row_spec, vec_spec, vec_spec]
        args += list(fused_in)
    in_specs.append(pl.BlockSpec(memory_space=pl.ANY))
    args.append(state_buf)
    scratch = [pltpu.VMEM((n_heads // 2, PAIR, SSD_STATE), F32)]
    if fused:
        scratch.append(pltpu.VMEM((CHUNK, d), F32))
    return pl.pallas_call(
        functools.partial(_ssd_kernel, reverse=reverse, fused=fused, n_heads=n_heads,
                          n_steps=n_steps),
        grid_spec=pltpu.PrefetchScalarGridSpec(
            num_scalar_prefetch=1,
            grid=(n_steps,),
            in_specs=in_specs,
            out_specs=[
                row_spec,
                pl.BlockSpec(state_block,
                             lambda s, tbl: (field(TBL_STATE)(s, tbl), l, direction, 0, 0, 0)),
            ],
            scratch_shapes=scratch,
        ),
        out_shape=[jax.ShapeDtypeStruct((t, d), BF16 if fused else F32),
                   jax.ShapeDtypeStruct(state_buf.shape, F32)],
        input_output_aliases={len(args) - 1: 1},
        compiler_params=_cparams(1, 40),
    )(*args)


def _ln_swish_store(conv, g_ref, b_ref, o_ref):
    x = conv[...]
    mu = jnp.mean(x, axis=-1, keepdims=True)
    xc = x - mu
    var = jnp.mean(xc * xc, axis=-1, keepdims=True)
    y = xc * lax.rsqrt(var + EPS) * g_ref[...] + b_ref[...]
    o_ref[...] = _silu(y).astype(o_ref.dtype)


def _cfconv_kernel(u_ref, w_ref, b_ref, g_ref, be_ref, o_ref, pad, conv, *,
                   seq, block, width, k, n_ctx_blocks):
    hk = k // 2
    lead = 16
    rows = block // GRID_W
    i = pl.program_id(0)

    @pl.when(i < n_ctx_blocks)
    def _():
        pad[0:lead, :] = jnp.zeros((lead, width), F32)
        pad[lead + seq:2 * lead + seq, :] = jnp.zeros((lead, width), F32)
        for sq in range(block // seq):
            pad[lead:lead + seq, :] = u_ref[sq * seq:(sq + 1) * seq, :]

            def body(cbi, carry):
                ds = pl.ds(pl.multiple_of(cbi * 128, 128), 128)
                acc = jnp.zeros((seq, 128), F32)
                for tap in range(k):
                    acc = acc + w_ref[tap:tap + 1, ds] * pad[pl.ds(lead - hk + tap, seq), ds]
                conv[pl.ds(sq * seq, seq), ds] = acc + b_ref[:, ds]
                return carry

            lax.fori_loop(0, width // 128, body, 0)

    @pl.when(i >= n_ctx_blocks)
    def _():
        def body(cbi, carry):
            ds = pl.ds(pl.multiple_of(cbi * 128, 128), 128)
            for r in range(rows):
                acc = None
                for tap in range(k):
                    src = r + tap - hk
                    if 0 <= src < rows:
                        term = w_ref[tap:tap + 1, ds] * u_ref[pl.ds(src * GRID_W, GRID_W), ds]
                        acc = term if acc is None else acc + term
                conv[pl.ds(r * GRID_W, GRID_W), ds] = acc + b_ref[:, ds]
            return carry

        lax.fori_loop(0, width // 128, body, 0)

    _ln_swish_store(conv, g_ref, be_ref, o_ref)


def _cfconv(u, w, b, g, be, *, t_ctx, seq, dec_seq):
    t, width = u.shape
    k = w.shape[0]
    assert dec_seq % seq == 0 and t_ctx % dec_seq == 0 and k // 2 < 16
    vec = pl.BlockSpec((1, width), lambda i: (0, 0))
    return pl.pallas_call(
        functools.partial(_cfconv_kernel, seq=seq, block=dec_seq, width=width, k=k,
                          n_ctx_blocks=t_ctx // dec_seq),
        grid=(t // dec_seq,),
        in_specs=[pl.BlockSpec((dec_seq, width), lambda i: (i, 0)),
                  pl.BlockSpec((k, width), lambda i: (0, 0)), vec, vec, vec],
        out_specs=pl.BlockSpec((dec_seq, width), lambda i: (i, 0)),
        out_shape=jax.ShapeDtypeStruct((t, width), BF16),
        scratch_shapes=[pltpu.VMEM((seq + 32, width), F32), pltpu.VMEM((dec_seq, width), F32)],
        compiler_params=_cparams(1, 48),
    )(u, w, b, g, be)


def _merge_kernel(y_ref, v_ref, u_ref, g0_ref, g1_ref, g2_ref, w0_ref, w1_ref, w2_ref,
                  o_ref, wb0, wb1, wb2):
    @pl.when(pl.program_id(1) == 0)
    def _():
        wb0[...] = w0_ref[...].astype(BF16)
        wb1[...] = w1_ref[...].astype(BF16)
        wb2[...] = w2_ref[...].astype(BF16)

    o0 = jnp.dot(y_ref[...], wb0[...], preferred_element_type=F32)
    o1 = jnp.dot(v_ref[...], wb1[...], preferred_element_type=F32)
    o2 = jnp.dot(u_ref[...], wb2[...], preferred_element_type=F32)
    merged = (g0_ref[...].astype(F32) * o0 + g1_ref[...].astype(F32) * o1
              + g2_ref[...].astype(F32) * o2)
    o_ref[...] = merged.astype(BF16)


def _merge(y, v, u, gates, ssd_out, sc_out, cf_out, l):
    t, d = y.shape
    kv, ku = v.shape[1], u.shape[1]
    tm, tn = 512, 512
    nj = d // tn
    one = pl.Buffered(1)

    def wspec(k):
        return pl.BlockSpec((None, k, tn), lambda j, i: (l, 0, j), pipeline_mode=one)

    return pl.pallas_call(
        _merge_kernel,
        grid=(nj, t // tm),
        in_specs=[
            pl.BlockSpec((tm, d), lambda j, i: (i, 0)),
            pl.BlockSpec((tm, kv), lambda j, i: (i, 0)),
            pl.BlockSpec((tm, ku), lambda j, i: (i, 0)),
            pl.BlockSpec((tm, tn), lambda j, i: (i, j)),
            pl.BlockSpec((tm, tn), lambda j, i: (i, nj + j)),
            pl.BlockSpec((tm, tn), lambda j, i: (i, 2 * nj + j)),
            wspec(d), wspec(kv), wspec(ku),
        ],
        out_specs=pl.BlockSpec((tm, tn), lambda j, i: (i, j)),
        out_shape=jax.ShapeDtypeStruct((t, d), BF16),
        scratch_shapes=[pltpu.VMEM((d, tn), BF16), pltpu.VMEM((kv, tn), BF16),
                        pltpu.VMEM((ku, tn), BF16)],
        compiler_params=_cparams(2, 48),
    )(y, v, u, gates, gates, gates, ssd_out, sc_out, cf_out)


def _wo_kernel(m_ref, w_ref, xa_ref, xb_ref, g_ref, o_ref, wb, *, tm, geo):
    i = pl.program_id(1)

    @pl.when(i == 0)
    def _():
        wb[...] = w_ref[...].astype(BF16)

    row = _mod_row(i * tm, *geo)
    out = g_ref[pl.ds(row, 1), :] * jnp.dot(m_ref[...], wb[...], preferred_element_type=F32)

    @pl.when(i * tm < geo[0])
    def _():
        o_ref[...] = xa_ref[...] + out

    @pl.when(i * tm >= geo[0])
    def _():
        o_ref[...] = xb_ref[...] + out


def _wo(merged, w_o, x_parts, mod, l, geo):
    t, d = merged.shape
    r = mod.shape[1]
    tm, tn = 512, 512
    nj = d // tn
    return pl.pallas_call(
        functools.partial(_wo_kernel, tm=tm, geo=geo),
        grid=(nj, t // tm),
        in_specs=[
            pl.BlockSpec((tm, d), lambda j, i: (i, 0)),
            pl.BlockSpec((None, d, tn), lambda j, i: (l, 0, j)),
        ] + _split_specs(x_parts, tm, tn, geo[0] // tm, 1, 0) + [
            pl.BlockSpec((None, r, tn), lambda j, i: (l, 0, 2 * nj + j)),
        ],
        out_specs=pl.BlockSpec((tm, tn), lambda j, i: (i, j)),
        out_shape=jax.ShapeDtypeStruct((t, d), F32),
        scratch_shapes=[pltpu.VMEM((d, tn), BF16)],
        compiler_params=_cparams(2, 48),
    )(merged, w_o, *x_parts, mod)


def _router_kernel(x_ref, g_ref, sh_ref, s_ref, rw_ref, rb_ref,
                   hp_ref, idx_ref, wt_ref, rank_ref, cnt_ref, carry, *, tm, geo, n_exp):
    i = pl.program_id(0)

    @pl.when(i == 0)
    def _():
        carry[...] = jnp.zeros(carry.shape, F32)

    row = _mod_row(i * tm, *geo)
    x = x_ref[...]
    y = x * lax.rsqrt(jnp.mean(x * x, axis=-1, keepdims=True) + EPS) * g_ref[...]
    h = y * (1.0 + s_ref[pl.ds(row, 1), :]) + sh_ref[pl.ds(row, 1), :]
    half = h.shape[1] // 2
    lo = lax.bitcast_convert_type(h[:, :half].astype(BF16).astype(F32), U32)
    hi = lax.bitcast_convert_type(h[:, half:].astype(BF16).astype(F32), U32)
    hp_ref[...] = (lo >> 16) | (hi & jnp.uint32(0xFFFF0000))

    logits = jnp.dot(h, rw_ref[...], precision=HIGHEST, preferred_element_type=F32)
    lt = logits.T[0:n_exp, :]
    score = _sigmoid(lt)
    sel = score + rb_ref[...]
    srow = [sel[e:e + 1, :] for e in range(n_exp)]
    prow = [score[e:e + 1, :] for e in range(n_exp)]

    def group_score(gi):
        v = srow[gi * GROUP_SIZE:(gi + 1) * GROUP_SIZE]
        best = None
        for a in range(GROUP_SIZE):
            for b in range(a + 1, GROUP_SIZE):
                s = v[a] + v[b]
                best = s if best is None else jnp.maximum(best, s)
        return best

    best_g = group_score(0)
    bi = jnp.zeros(best_g.shape, I32)
    for gi in range(1, N_EXPERT_GROUPS):
        gs = group_score(gi)
        upd = gs > best_g
        best_g = jnp.where(upd, gs, best_g)
        bi = jnp.where(upd, gi, bi)

    def pick(rows, j):
        out = rows[j]
        for gi in range(1, N_EXPERT_GROUPS):
            out = jnp.where(bi == gi, rows[gi * GROUP_SIZE + j], out)
        return out

    v = [pick(srow, j) for j in range(GROUP_SIZE)]
    p = [pick(prow, j) for j in range(GROUP_SIZE)]

    def argmax_first(vals):
        best, arg = vals[0], jnp.zeros(vals[0].shape, I32)
        for j in range(1, GROUP_SIZE):
            upd = vals[j] > best
            best = jnp.where(upd, vals[j], best)
            arg = jnp.where(upd, j, arg)
        return arg

    i1 = argmax_first(v)
    i2 = argmax_first([jnp.where(i1 == j, -jnp.inf, v[j]) for j in range(GROUP_SIZE)])
    p1 = sum(jnp.where(i1 == j, p[j], 0.0) for j in range(GROUP_SIZE))
    p2 = sum(jnp.where(i2 == j, p[j], 0.0) for j in range(GROUP_SIZE))
    den = p1 + p2
    e1 = bi * GROUP_SIZE + i1
    e2 = bi * GROUP_SIZE + i2
    idx_ref[0:1, :] = e1
    idx_ref[1:2, :] = e2
    wt_ref[0:1, :] = p1 / den
    wt_ref[1:2, :] = p2 / den

    erow = lax.broadcasted_iota(I32, (n_exp, tm), 0)
    oh1 = (erow == e1).astype(F32)
    oh2 = (erow == e2).astype(F32)
    cnt = oh1 + oh2
    before = (lax.broadcasted_iota(I32, (tm, tm), 0)
              < lax.broadcasted_iota(I32, (tm, tm), 1)).astype(BF16)
    prior = jnp.dot(cnt.astype(BF16), before, preferred_element_type=F32) + carry[...]
    rank_ref[0:1, :] = jnp.sum(oh1 * prior, axis=0, keepdims=True).astype(I32)
    rank_ref[1:2, :] = jnp.sum(oh2 * prior, axis=0, keepdims=True).astype(I32)
    carry[...] = carry[...] + jnp.sum(cnt, axis=1, keepdims=True)
    cnt_ref[...] = jnp.broadcast_to(carry[...], cnt_ref.shape)


def _router(x, g, mod, l, rw_pad, rb_col, geo):
    t, d = x.shape
    r = mod.shape[1]
    n_exp = rb_col.shape[0]
    tm = 256
    return pl.pallas_call(
        functools.partial(_router_kernel, tm=tm, geo=geo, n_exp=n_exp),
        grid=(t // tm,),
        in_specs=[
            pl.BlockSpec((tm, d), lambda i: (i, 0)),
            pl.BlockSpec((1, d), lambda i: (0, 0)),
            pl.BlockSpec((None, r, d), lambda i: (l, 0, 3)),
            pl.BlockSpec((None, r, d), lambda i: (l, 0, 4)),
            pl.BlockSpec(rw_pad.shape, lambda i: (0, 0)),
            pl.BlockSpec(rb_col.shape, lambda i: (0, 0)),
        ],
        out_specs=[
            pl.BlockSpec((tm, d // 2), lambda i: (i, 0)),
            pl.BlockSpec((2, tm), lambda i: (0, i)),
            pl.BlockSpec((2, tm), lambda i: (0, i)),
            pl.BlockSpec((2, tm), lambda i: (0, i)),
            pl.BlockSpec((n_exp, 128), lambda i: (0, 0)),
        ],
        out_shape=[
            jax.ShapeDtypeStruct((t, d // 2), U32),
            jax.ShapeDtypeStruct((2, t), I32),
            jax.ShapeDtypeStruct((2, t), F32),
            jax.ShapeDtypeStruct((2, t), I32),
            jax.ShapeDtypeStruct((n_exp, 128), F32),
        ],
        scratch_shapes=[pltpu.VMEM((n_exp, 1), F32)],
        compiler_params=_cparams(1, 32),
    )(x, g, mod, mod, rw_pad, rb_col)


def _row_copy(src, dst, sem):
    return pltpu.make_async_copy(src, dst, sem)


def _dispatch_kernel(idx_s, rank_s, off_s, h_ref, xs_in, xs_out, sem, *, tm, t):
    del xs_in
    base = pl.program_id(0) * tm

    def issue(r, carry):
        for k in range(2):
            pos = k * t + base + r
            dst = off_s[idx_s[pos]] + rank_s[pos]
            _row_copy(h_ref.at[pl.ds(r, 1)], xs_out.at[pl.ds(dst, 1)], sem).start()
        return carry

    lax.fori_loop(0, tm, issue, 0, unroll=4)

    def drain(r, carry):
        for k in range(2):
            _row_copy(h_ref.at[pl.ds(0, 1)], xs_out.at[pl.ds(0, 1)], sem).wait()
        return carry

    lax.fori_loop(0, tm, drain, 0, unroll=4)


def _dispatch(hp, idx_flat, rank_flat, off, n_rows):
    t, w = hp.shape
    tm = 256
    xs0 = jnp.zeros((n_rows, w), U32)
    return pl.pallas_call(
        functools.partial(_dispatch_kernel, tm=tm, t=t),
        grid_spec=pltpu.PrefetchScalarGridSpec(
            num_scalar_prefetch=3,
            grid=(t // tm,),
            in_specs=[pl.BlockSpec((tm, w), lambda i, *_: (i, 0)),
                      pl.BlockSpec(memory_space=pl.ANY)],
            out_specs=pl.BlockSpec(memory_space=pl.ANY),
            scratch_shapes=[pltpu.SemaphoreType.DMA(())],
        ),
        out_shape=jax.ShapeDtypeStruct((n_rows, w), U32),
        input_output_aliases={4: 0},
        compiler_params=_cparams(1, 32),
    )(idx_flat, rank_flat, off, hp, xs0)


def _expert_changed(te_s, i):
    return (i == 0) | (te_s[i] != te_s[jnp.maximum(i - 1, 0)])


def _up_kernel(te_s, nu_s, x_ref, wg_ref, wu_ref, o_ref, wgb, wub):
    i = pl.program_id(1)

    @pl.when(_expert_changed(te_s, i))
    def _():
        wgb[...] = wg_ref[...].astype(BF16)
        wub[...] = wu_ref[...].astype(BF16)

    @pl.when(i < nu_s[0])
    def _():
        pk = x_ref[...]
        lo = lax.bitcast_convert_type(pk << 16, F32)
        hi = lax.bitcast_convert_type(pk & jnp.uint32(0xFFFF0000), F32)
        x = jnp.concatenate([lo, hi], axis=1).astype(BF16)
        gt = jnp.dot(x, wgb[...], preferred_element_type=F32)
        up = jnp.dot(x, wub[...], preferred_element_type=F32)
        o_ref[...] = (_silu(gt) * up).astype(BF16)

    @pl.when(i >= nu_s[0])
    def _():
        o_ref[...] = jnp.zeros(o_ref.shape, BF16)


def _moe_up(xs, te, nu, w_gate, w_up, l, tm):
    n_rows, w = xs.shape
    d, f = w_gate.shape[2], w_gate.shape[3]
    tf = 256
    wspec = pl.BlockSpec((None, None, d, tf), lambda j, i, te_s, nu_s: (l, te_s[i], 0, j))
    return pl.pallas_call(
        _up_kernel,
        grid_spec=pltpu.PrefetchScalarGridSpec(
            num_scalar_prefetch=2,
            grid=(f // tf, n_rows // tm),
            in_specs=[pl.BlockSpec((tm, w), lambda j, i, *_: (i, 0)), wspec, wspec],
            out_specs=pl.BlockSpec((tm, tf), lambda j, i, *_: (i, j)),
            scratch_shapes=[pltpu.VMEM((d, tf), BF16)] * 2,
        ),
        out_shape=jax.ShapeDtypeStruct((n_rows, f), BF16),
        compiler_params=_cparams(2, 48),
    )(te, nu, xs, w_gate, w_up)


def _down_kernel(te_s, nu_s, a_ref, w_ref, o_ref, wb):
    i = pl.program_id(1)

    @pl.when(_expert_changed(te_s, i))
    def _():
        wb[...] = w_ref[...].astype(BF16)

    @pl.when(i < nu_s[0])
    def _():
        o_ref[...] = jnp.dot(a_ref[...], wb[...], preferred_element_type=F32)

    @pl.when(i >= nu_s[0])
    def _():
        o_ref[...] = jnp.zeros(o_ref.shape, F32)


def _moe_down(act, te, nu, w_down, l, tm):
    n_rows, f = act.shape
    d = w_down.shape[3]
    tn = min(d, 2048)
    return pl.pallas_call(
        _down_kernel,
        grid_spec=pltpu.PrefetchScalarGridSpec(
            num_scalar_prefetch=2,
            grid=(d // tn, n_rows // tm),
            in_specs=[pl.BlockSpec((tm, f), lambda j, i, *_: (i, 0)),
                      pl.BlockSpec((None, None, f, tn), lambda j, i, te_s, nu_s: (l, te_s[i], 0, j))],
            out_specs=pl.BlockSpec((tm, tn), lambda j, i, *_: (i, j)),
            scratch_shapes=[pltpu.VMEM((f, tn), BF16)],
        ),
        out_shape=jax.ShapeDtypeStruct((n_rows, d), F32),
        compiler_params=_cparams(2, 48),
    )(te, nu, act, w_down)


def _combine_kernel(idx_s, rank_s, off_s, x_ref, wt_ref, g2_ref, ng_ref, sh_ref, s_ref, ys_ref,
                    *rest, tm, t, geo, final):
    o_ref, o2_ref, buf, sem = rest
    i = pl.program_id(0)
    n_steps = t // tm
    slot = lax.rem(i, 2)

    def gather(tile, dst_slot):
        def issue(r, carry):
            for k in range(2):
                pos = k * t + tile * tm + r
                src = off_s[idx_s[pos]] + rank_s[pos]
                _row_copy(ys_ref.at[pl.ds(src, 1)], buf.at[dst_slot, k, pl.ds(r, 1)],
                          sem.at[dst_slot]).start()
            return carry

        lax.fori_loop(0, tm, issue, 0, unroll=4)

    @pl.when(i == 0)
    def _():
        gather(0, 0)

    @pl.when(i + 1 < n_steps)
    def _():
        gather(i + 1, 1 - slot)

    def drain(r, carry):
        for k in range(2):
            _row_copy(ys_ref.at[pl.ds(0, 1)], buf.at[slot, k, pl.ds(0, 1)], sem.at[slot]).wait()
        return carry

    lax.fori_loop(0, tm, drain, 0, unroll=4)

    row = _mod_row(i * tm, *geo)
    wt = wt_ref[...]
    moe = wt[:, 0:1] * buf[slot, 0] + wt[:, 1:2] * buf[slot, 1]
    xn = x_ref[...] + g2_ref[pl.ds(row, 1), :] * moe
    y = xn * lax.rsqrt(jnp.mean(xn * xn, axis=-1, keepdims=True) + EPS) * ng_ref[...]
    if final:
        @pl.when(i * tm < geo[0])
        def _():
            o_ref[...] = y

        @pl.when(i * tm >= geo[0])
        def _():
            o2_ref[...] = y
    else:
        o_ref[...] = xn
        o2_ref[...] = (y * (1.0 + s_ref[pl.ds(row, 1), :]) + sh_ref[pl.ds(row, 1), :]).astype(BF16)


def _combine(x, wt_col, mod, l, norm_g, ys, idx_flat, rank_flat, off, geo, final):
    t, d = x.shape
    r = mod.shape[1]
    tm = 256
    l_next = l if final else l + 1
    if final:
        n_ctx_tiles = geo[0] // tm
        out_specs = [pl.BlockSpec((tm, d), lambda i, *_: (jnp.minimum(i, n_ctx_tiles - 1), 0)),
                     pl.BlockSpec((tm, d), lambda i, *_: (jnp.maximum(i - n_ctx_tiles, 0), 0))]
        out_shape = [jax.ShapeDtypeStruct((geo[0], d), F32),
                     jax.ShapeDtypeStruct((t - geo[0], d), F32)]
    else:
        out_specs = [pl.BlockSpec((tm, d), lambda i, *_: (i, 0))] * 2
        out_shape = [jax.ShapeDtypeStruct((t, d), F32), jax.ShapeDtypeStruct((t, d), BF16)]
    return pl.pallas_call(
        functools.partial(_combine_kernel, tm=tm, t=t, geo=geo, final=final),
        grid_spec=pltpu.PrefetchScalarGridSpec(
            num_scalar_prefetch=3,
            grid=(t // tm,),
            in_specs=[
                pl.BlockSpec((tm, d), lambda i, *_: (i, 0)),
                pl.BlockSpec((tm, 2), lambda i, *_: (i, 0)),
                pl.BlockSpec((None, r, d), lambda i, *_: (l, 0, 5)),
                pl.BlockSpec((1, d), lambda i, *_: (0, 0)),
                pl.BlockSpec((None, r, d), lambda i, *_: (l_next, 0, 0)),
                pl.BlockSpec((None, r, d), lambda i, *_: (l_next, 0, 1)),
                pl.BlockSpec(memory_space=pl.ANY),
            ],
            out_specs=out_specs,
            scratch_shapes=[pltpu.VMEM((2, 2, tm, d), F32), pltpu.SemaphoreType.DMA((2,))],
        ),
        out_shape=out_shape,
        compiler_params=_cparams(1, 48),
    )(idx_flat, rank_flat, off, x, wt_col, mod, norm_g, mod, mod, ys)


def kernel(x_prompt, x_sample, state_ssd, c, c_ctx, ada_w, ada_b, norm1_g, norm2_g, w_in,
           ssd_conv_w, ssd_conv_b, ssd_dt_bias, ssd_a_log, ssd_d, ssd_norm_g, ssd_out,
           sc_conv_w, sc_conv_b, sc_out, cf_conv_w, cf_conv_b, cf_ln_g, cf_ln_b, cf_out, w_o,
           router_w, router_bias, moe_w_gate, moe_w_up, moe_w_down, final_g):
    n_ctx, seq, d = x_prompt.shape
    n_dec, dec_seq, _ = x_sample.shape
    depth = w_in.shape[0]
    n_heads = d // HEAD_DIM
    gn = SSD_GROUPS * SSD_STATE
    xbc_w = d + 2 * gn
    sc_w = sc_out.shape[1]
    cf_w = cf_out.shape[1]
    n_exp = router_w.shape[1]
    t_ctx = n_ctx * seq
    t = t_ctx + n_dec * dec_seq
    geo = (t_ctx, dec_seq, n_dec)
    off_xbc = d
    off_dt = off_xbc + xbc_w
    off_sc = off_dt + 2 * n_heads
    off_cf = off_sc + 3 * sc_w
    off_gate = off_cf + 2 * cf_w

    x_parts = (x_prompt.reshape(t_ctx, d), x_sample.reshape(n_dec * dec_seq, d))
    n_mod_rows = -(-(n_dec + 1) // 8) * 8
    cond = jnp.concatenate([c, c_ctx[None, :],
                            jnp.zeros((n_mod_rows - n_dec - 1, d), F32)], axis=0)
    mod = _ada_table(cond, ada_w, ada_b)

    h0_lat = state_ssd.reshape(n_dec, depth, 2, n_heads // 2, PAIR, SSD_STATE)
    state_buf = jnp.zeros((n_ctx, depth, 2, n_heads // 2, PAIR, SSD_STATE), F32)
    rw_pad = jnp.pad(router_w, ((0, 0), (0, 128 - n_exp)))
    rb_col = router_bias.reshape(n_exp, 1)
    tm_e = 512
    n_rows = 2 * t + n_exp * tm_e
    n_tiles = n_rows // tm_e

    h = _normmod(x_parts, norm1_g[0:1], mod, 0, geo)
    y_final = None
    for l in range(depth):
        z = _proj(h, w_in, l, [0], d, 512, 1024, _epi_identity, [], [], F32)
        xbc = _proj(
            h, w_in, l, [off_xbc], xbc_w, 512, dec_seq,
            functools.partial(_epi_xbc, tm=dec_seq, t_ctx=t_ctx, seq=seq, dec_seq=dec_seq),
            [ssd_conv_w, ssd_conv_b.reshape(depth, 1, xbc_w)],
            [pl.BlockSpec((None, 3, 512), lambda j, i: (l, 0, j)),
             pl.BlockSpec((None, 1, 512), lambda j, i: (l, 0, j))], F32)
        dt = _proj(h, w_in, l, [off_dt], 2 * n_heads, 2 * n_heads, 1024, _epi_dt,
                   [ssd_dt_bias.reshape(depth, 1, 2 * n_heads)],
                   [pl.BlockSpec((None, 1, 2 * n_heads), lambda j, i: (l, 0, 0))], F32)
        v = _proj(
            h, w_in, l, [off_sc, off_sc + sc_w, off_sc + 2 * sc_w], sc_w, 256, 512,
            functools.partial(_epi_sc, tm=512, t_ctx=t_ctx, seq=seq),
            [sc_conv_w, sc_conv_b.reshape(depth, 1, sc_w)],
            [pl.BlockSpec((None, 3, 256), lambda j, i: (l, 0, j)),
             pl.BlockSpec((None, 1, 256), lambda j, i: (l, 0, j))], BF16)
        u_cf = _proj(h, w_in, l, [off_cf, off_cf + cf_w], cf_w, 256, 1024, _epi_cf, [], [], F32)
        gates = _proj(h, w_in, l, [off_gate], 3 * d, 512, 1024, _epi_sigmoid, [], [], BF16)

        alog = ssd_a_log[l].reshape(1, 2 * n_heads)
        dskip = jnp.repeat(ssd_d[l], HEAD_DIM).reshape(1, d)
        ng = ssd_norm_g[l:l + 1]
        scan = dict(d=d, l=l, shape=(n_ctx, seq, n_dec, dec_seq))
        yf, state_buf = _ssd_pass(xbc, dt, alog, h0_lat, state_buf, reverse=False, **scan)
        y_ssd, state_buf = _ssd_pass(xbc, dt, alog, h0_lat, state_buf, reverse=True,
                                     fused_in=(yf, z, dskip, ng), **scan)

        u = _cfconv(u_cf, cf_conv_w[l], cf_conv_b[l:l + 1], cf_ln_g[l:l + 1], cf_ln_b[l:l + 1],
                    t_ctx=t_ctx, seq=seq, dec_seq=dec_seq)

        merged = _merge(y_ssd, v, u, gates, ssd_out, sc_out, cf_out, l)
        x = _wo(merged, w_o, x_parts, mod, l, geo)

        hp, idx, wt, rank, cnt = _router(x, norm2_g[l:l + 1], mod, l, rw_pad, rb_col, geo)
        counts = cnt[:, 0].astype(I32)
        tiles = (counts + tm_e - 1) // tm_e
        tile_end = jnp.cumsum(tiles)
        off = (tile_end - tiles) * tm_e
        n_used = tile_end[-1]
        tile_id = jnp.minimum(jnp.arange(n_tiles, dtype=I32), n_used - 1)
        te = jnp.sum(tile_id[:, None] >= tile_end[None, :], axis=1).astype(I32)
        nu = n_used.reshape(1).astype(I32)
        idx_flat = idx.reshape(2 * t)
        rank_flat = rank.reshape(2 * t)
        xs = _dispatch(hp, idx_flat, rank_flat, off, n_rows)
        act = _moe_up(xs, te, nu, moe_w_gate, moe_w_up, l, tm_e)
        ys = _moe_down(act, te, nu, moe_w_down, l, tm_e)
        final = l == depth - 1
        ng_next = final_g.reshape(1, d) if final else norm1_g[l + 1:l + 2]
        outs = _combine(x, wt.T, mod, l, ng_next, ys, idx_flat, rank_flat, off, geo, final)
        if final:
            y_final = outs
        else:
            x, h = outs
            x_parts = (x, x)

    y_prompt = y_final[0].reshape(n_ctx, seq, d)
    y_sample = y_final[1].reshape(n_dec, dec_seq, d)
    new_state = state_buf.reshape(n_ctx, depth, 2, n_heads, HEAD_DIM, SSD_STATE)
    return (y_prompt, y_sample, new_state)
```

```python
import functools

import jax
import jax.numpy as jnp
import numpy as np
from jax import lax
from jax.experimental import pallas as pl
from jax.experimental.pallas import tpu as pltpu

F32 = jnp.float32
BF16 = jnp.bfloat16
I32 = jnp.int32
U32 = jnp.uint32

EPS = 1e-6
GRID_W = 64
HEAD_DIM = 64
PAIR = 2 * HEAD_DIM
SSD_GROUPS = 8
SSD_STATE = 128
CHUNK = 128
N_EXPERT_GROUPS = 4
GROUP_SIZE = 4
MIB = 1024 * 1024
VMEM_BUDGET_MIB = 56


def _cparams(n_axes, vmem_mib):
    return pltpu.CompilerParams(
        dimension_semantics=("arbitrary",) * n_axes,
        vmem_limit_bytes=min(vmem_mib, VMEM_BUDGET_MIB) * MIB)


def _sigmoid(x):
    return 1.0 / (1.0 + jnp.exp(-x))


def _silu(x):
    return x * _sigmoid(x)


def _mod_row(start, t_ctx, dec_seq, n_dec):
    return jnp.where(start < t_ctx, n_dec, lax.div(start - t_ctx, dec_seq))


def _dot_split(a, b):
    a_hi, b_hi = a.astype(BF16), b.astype(BF16)
    a_lo = (a - a_hi.astype(F32)).astype(BF16)
    b_lo = (b - b_hi.astype(F32)).astype(BF16)
    return (jnp.dot(a_hi, b_hi, preferred_element_type=F32)
            + jnp.dot(a_lo, b_hi, preferred_element_type=F32)
            + jnp.dot(a_hi, b_lo, preferred_element_type=F32))


def _ada_kernel(c_ref, w_ref, b_ref, o_ref):
    o_ref[...] = _dot_split(_silu(c_ref[...]), w_ref[...]) + b_ref[...]


def _ada_table(cond, ada_w, ada_b):
    depth, d, n = ada_w.shape
    r = cond.shape[0]
    tn = 512
    return pl.pallas_call(
        _ada_kernel,
        grid=(depth, n // tn),
        in_specs=[
            pl.BlockSpec((r, d), lambda l, j: (0, 0)),
            pl.BlockSpec((None, d, tn), lambda l, j: (l, 0, j)),
            pl.BlockSpec((None, 1, tn), lambda l, j: (l, 0, j)),
        ],
        out_specs=pl.BlockSpec((None, r, tn), lambda l, j: (l, 0, j)),
        out_shape=jax.ShapeDtypeStruct((depth, r, n), F32),
        compiler_params=_cparams(2, 32),
    )(cond, ada_w, ada_b.reshape(depth, 1, n))


def _split_specs(x_parts, tm, width, n_ctx_tiles, tile_axis, col_axis=None):
    xa, xb = x_parts
    b_off = n_ctx_tiles if xa is xb else 0

    def col(ids):
        return 0 if col_axis is None else ids[col_axis]

    def amap(*ids):
        return (jnp.minimum(ids[tile_axis], n_ctx_tiles - 1), col(ids))

    def bmap(*ids):
        return (b_off + jnp.maximum(ids[tile_axis] - n_ctx_tiles, 0), col(ids))

    return [pl.BlockSpec((tm, width), amap), pl.BlockSpec((tm, width), bmap)]


def _normmod_kernel(xa_ref, xb_ref, g_ref, sh_ref, s_ref, o_ref, *, tm, geo):
    start = pl.program_id(0) * tm
    row = _mod_row(start, *geo)

    def emit(x_ref):
        x = x_ref[...]
        y = x * lax.rsqrt(jnp.mean(x * x, axis=-1, keepdims=True) + EPS) * g_ref[...]
        o_ref[...] = (y * (1.0 + s_ref[pl.ds(row, 1), :]) + sh_ref[pl.ds(row, 1), :]).astype(BF16)

    pl.when(start < geo[0])(lambda: emit(xa_ref))
    pl.when(start >= geo[0])(lambda: emit(xb_ref))


def _normmod(x_parts, g, mod, l, geo):
    d = x_parts[0].shape[1]
    t = geo[0] + geo[1] * geo[2]
    r = mod.shape[1]
    tm = 256
    return pl.pallas_call(
        functools.partial(_normmod_kernel, tm=tm, geo=geo),
        grid=(t // tm,),
        in_specs=_split_specs(x_parts, tm, d, geo[0] // tm, 0) + [
            pl.BlockSpec((1, d), lambda i: (0, 0)),
            pl.BlockSpec((None, r, d), lambda i: (l, 0, 0)),
            pl.BlockSpec((None, r, d), lambda i: (l, 0, 1)),
        ],
        out_specs=pl.BlockSpec((tm, d), lambda i: (i, 0)),
        out_shape=jax.ShapeDtypeStruct((t, d), BF16),
        compiler_params=_cparams(1, 32),
    )(*x_parts, g, mod, mod)


def _proj_kernel(*refs, n_dots, n_extra, epilogue):
    h_ref = refs[0]
    w_refs = refs[1:1 + n_dots]
    extra = refs[1 + n_dots:1 + n_dots + n_extra]
    o_ref = refs[1 + n_dots + n_extra]
    wb = refs[2 + n_dots + n_extra:]
    i = pl.program_id(1)

    @pl.when(i == 0)
    def _():
        for d in range(n_dots):
            wb[d][...] = w_refs[d][...].astype(BF16)

    h = h_ref[...]
    accs = [jnp.dot(h, wb[d][...], preferred_element_type=F32) for d in range(n_dots)]
    o_ref[...] = epilogue(accs, extra, i).astype(o_ref.dtype)


def _proj(h, w_in, l, col_offs, n_cols, tn, tm, epilogue, extra, extra_specs, out_dtype):
    t, k = h.shape
    n_dots = len(col_offs)

    def wspec(off):
        return pl.BlockSpec((pl.Squeezed(), pl.Element(k), pl.Element(tn)),
                            lambda j, i: (l, 0, pl.multiple_of(off + j * tn, 128)))

    out_bytes = jnp.dtype(out_dtype).itemsize
    vmem = (2 * tm * k * 2 + n_dots * k * tn * (2 * 4 + 2) + 2 * tm * tn * out_bytes) // MIB + 8
    return pl.pallas_call(
        functools.partial(_proj_kernel, n_dots=n_dots, n_extra=len(extra), epilogue=epilogue),
        grid=(n_cols // tn, t // tm),
        in_specs=[pl.BlockSpec((tm, k), lambda j, i: (i, 0))]
        + [wspec(off) for off in col_offs] + list(extra_specs),
        out_specs=pl.BlockSpec((tm, tn), lambda j, i: (i, j)),
        out_shape=jax.ShapeDtypeStruct((t, n_cols), out_dtype),
        scratch_shapes=[pltpu.VMEM((k, tn), BF16)] * n_dots,
        compiler_params=_cparams(2, vmem),
    )(h, *([w_in] * n_dots), *extra)


def _conv3_rows(u, w_ref, b_ref, period):
    rows = u.shape[0]
    pos = lax.broadcasted_iota(I32, u.shape, 0) & (period - 1)
    prev = jnp.where(pos == 0, 0.0, pltpu.roll(u, 1, 0))
    nxt = jnp.where(pos == period - 1, 0.0, pltpu.roll(u, rows - 1, 0))
    return w_ref[0:1, :] * prev + w_ref[1:2, :] * u + w_ref[2:3, :] * nxt + b_ref[...]


def _epi_identity(accs, extra, i):
    return accs[0]


def _epi_sigmoid(accs, extra, i):
    return _sigmoid(accs[0])


def _epi_xbc(accs, extra, i, *, tm, t_ctx, seq, dec_seq):
    period = jnp.where(i * tm < t_ctx, seq, dec_seq)
    return _silu(_conv3_rows(accs[0], extra[0], extra[1], period))


def _epi_dt(accs, extra, i):
    x = accs[0] + extra[0][...]
    return jnp.maximum(x, 0.0) + jnp.log1p(jnp.exp(-jnp.abs(x)))


def _epi_sc(accs, extra, i, *, tm, t_ctx, seq):
    bg, cg, hv = accs
    period = jnp.where(i * tm < t_ctx, seq, GRID_W)
    return bg * _conv3_rows(cg * hv, extra[0], extra[1], period)


def _epi_cf(accs, extra, i):
    return accs[0] * _sigmoid(accs[1])


TBL_BLOCK, TBL_FIRST, TBL_LAST, TBL_LATENT, TBL_H0, TBL_STATE, TBL_FIELDS = 0, 1, 2, 3, 4, 5, 6


def _ssd_kernel(tbl, *refs, reverse, fused, n_heads, n_steps):
    it = iter(refs)
    xs_ref, b_ref, c_ref, dt_ref, alog_ref, expand_ref, h0_ref = (next(it) for _ in range(7))
    if fused:
        yf_ref, z_ref, dskip_ref, ng_ref = (next(it) for _ in range(4))
    next(it)
    y_ref, st_ref, h_scr = next(it), next(it), next(it)
    ybuf = next(it) if fused else None

    step = pl.program_id(0)
    first = tbl[TBL_FIRST * n_steps + step] == 1
    latent = tbl[TBL_LATENT * n_steps + step] == 1

    @pl.when(first & latent)
    def _():
        h_scr[...] = h0_ref[...]

    @pl.when(first & jnp.logical_not(latent))
    def _():
        h_scr[...] = jnp.zeros(h_scr.shape, F32)

    q = CHUNK
    dt = dt_ref[...]
    dta = dt * (-jnp.exp(alog_ref[...]))
    r_i = lax.broadcasted_iota(I32, (q, q), 0)
    c_i = lax.broadcasted_iota(I32, (q, q), 1)
    tri = (c_i >= r_i) if reverse else (c_i <= r_i)
    d_hi = dta.astype(BF16)
    rem = dta - d_hi.astype(F32)
    d_mid = rem.astype(BF16)
    d_lo = (rem - d_mid.astype(F32)).astype(BF16)
    parts = jnp.dot(tri.astype(BF16), jnp.concatenate([d_hi, d_mid, d_lo], axis=1),
                    preferred_element_type=F32)
    nh2 = dta.shape[1]
    acum = parts[:, 0:nh2] + parts[:, nh2:2 * nh2] + parts[:, 2 * nh2:3 * nh2]
    acum_t = acum.T
    last = acum[0:1, :] if reverse else acum[q - 1:q, :]
    expa = jnp.exp(acum)
    dend = jnp.exp(last - acum)
    cdec = jnp.exp(last)
    lane_lo = lax.broadcasted_iota(I32, (q, PAIR), 1) < HEAD_DIM
    row_lo = lax.broadcasted_iota(I32, (PAIR, SSD_STATE), 0) < HEAD_DIM
    d0 = n_heads if reverse else 0
    hg = n_heads // SSD_GROUPS
    nt = (((1,), (1,)), ((), ()))

    per_head = jnp.concatenate([dt, dt * dend, expa], axis=0)
    hi = per_head.astype(BF16)
    lo = (per_head - hi.astype(F32)).astype(BF16)
    on_lanes = jnp.dot(jnp.concatenate([hi, lo], axis=1), expand_ref[...],
                       preferred_element_type=F32)
    xs_all = xs_ref[...]
    xdt_all = xs_all * on_lanes[0:q]
    xw_all = xs_all * on_lanes[q:2 * q]
    expa_all = on_lanes[2 * q:3 * q]

    pairs_g = hg // 2
    for g in range(SSD_GROUPS):
        cg = c_ref[:, g * SSD_STATE:(g + 1) * SSD_STATE].astype(BF16)
        bg = b_ref[:, g * SSD_STATE:(g + 1) * SSD_STATE].astype(BF16)
        cb = lax.dot_general(cg, bg, nt, preferred_element_type=F32)
        p0 = g * pairs_g
        gslab = slice(p0 * PAIR, (p0 + pairs_g) * PAIR)
        hgrp = jnp.concatenate([h_scr[p0 + jp] for jp in range(pairs_g)], axis=0)
        y_in = lax.dot_general(cg, hgrp.astype(BF16), nt,
                               preferred_element_type=F32) * expa_all[:, gslab]
        st = jnp.dot(xw_all[:, gslab].T.astype(BF16), bg, preferred_element_type=F32)
        for jp in range(pairs_g):
            pair = p0 + jp
            hx = d0 + 2 * pair
            rows = slice(jp * PAIR, (jp + 1) * PAIR)
            keep = jnp.where(row_lo, cdec[:, hx:hx + 1], cdec[:, hx + 1:hx + 2])
            h_scr[pair] = keep * hgrp[rows] + st[rows]

            def decay_w(hh):
                diff = acum[:, hh:hh + 1] - acum_t[hh:hh + 1, :]
                return (cb * jnp.exp(jnp.where(tri, diff, -jnp.inf))).astype(BF16)

            slab = slice(pair * PAIR, (pair + 1) * PAIR)
            w = jnp.concatenate([decay_w(hx), decay_w(hx + 1)], axis=1)
            xdt = xdt_all[:, slab]
            xbd = jnp.concatenate([jnp.where(lane_lo, xdt, 0.0),
                                   jnp.where(lane_lo, 0.0, xdt)], axis=0).astype(BF16)
            y = jnp.dot(w, xbd, preferred_element_type=F32) + y_in[:, rows]
            if fused:
                ybuf[:, slab] = y
            else:
                y_ref[:, slab] = y

    @pl.when((tbl[TBL_LAST * n_steps + step] == 1) & jnp.logical_not(latent))
    def _():
        st_ref[...] = h_scr[...]

    if fused:
        yt = ybuf[...] + yf_ref[...] + dskip_ref[...] * xs_all
        yt = yt * _silu(z_ref[...])
        yn = yt * lax.rsqrt(jnp.mean(yt * yt, axis=-1, keepdims=True) + EPS) * ng_ref[...]
        y_ref[...] = yn.astype(BF16)


def _ssd_table(n_ctx, seq, n_dec, dec_seq, reverse):
    steps = []
    blk = 0
    for n_seq, length, latent in ((n_ctx, seq, 0), (n_dec, dec_seq, 1)):
        n_chunks = length // CHUNK
        for b in range(n_seq):
            for c in range(n_chunks):
                steps.append((blk + c, int(c == 0), int(c == n_chunks - 1), latent,
                              b if latent else 0, n_ctx - 1 if latent else b))
            blk += n_chunks
    if reverse:
        steps = [(s[0], s[2], s[1], s[3], s[4], s[5]) for s in reversed(steps)]
    return np.asarray(steps, np.int32).T.reshape(-1), len(steps)


def _ssd_pass(xbc, dt, alog, h0, state_buf, *, d, l, shape, reverse, fused_in=None):
    n_heads = d // HEAD_DIM
    gn = SSD_GROUPS * SSD_STATE
    direction = 1 if reverse else 0
    fused = fused_in is not None
    t = xbc.shape[0]
    table, n_steps = _ssd_table(*shape, reverse)

    def field(f):
        return lambda s, tbl: tbl[f * n_steps + s]

    blk = field(TBL_BLOCK)
    row_spec = pl.BlockSpec((CHUNK, d), lambda s, tbl: (blk(s, tbl), 0))
    vec_spec = pl.BlockSpec((1, d), lambda s, tbl: (0, 0))
    state_block = (None, None, None, n_heads // 2, PAIR, SSD_STATE)
    in_specs = [
        row_spec,
        pl.BlockSpec((CHUNK, gn), lambda s, tbl: (blk(s, tbl), d // gn)),
        pl.BlockSpec((CHUNK, gn), lambda s, tbl: (blk(s, tbl), d // gn + 1)),
        pl.BlockSpec((CHUNK, 2 * n_heads), lambda s, tbl: (blk(s, tbl), 0)),
        pl.BlockSpec((1, 2 * n_heads), lambda s, tbl: (0, 0)),
        pl.BlockSpec((4 * n_heads, d), lambda s, tbl: (0, 0)),
        pl.BlockSpec(state_block,
                     lambda s, tbl: (field(TBL_H0)(s, tbl), l, direction, 0, 0, 0)),
    ]
    expand = np.zeros((2 * n_heads, d), np.float32)
    expand[direction * n_heads + np.arange(d) // HEAD_DIM, np.arange(d)] = 1.0
    expand = np.concatenate([expand, expand], axis=0)
    args = [jnp.asarray(table), xbc, xbc, xbc, dt, alog, jnp.asarray(expand, BF16), h0]
    if fused:
        in_specs += [row_spec, row_spec, vec_spec, vec_spec]
        args += list(fused_in)
    in_specs.append(pl.BlockSpec(memory_space=pl.ANY))
    args.append(state_buf)
    scratch = [pltpu.VMEM((n_heads // 2, PAIR, SSD_STATE), F32)]
    if fused:
        scratch.append(pltpu.VMEM((CHUNK, d), F32))
    return pl.pallas_call(
        functools.partial(_ssd_kernel, reverse=reverse, fused=fused, n_heads=n_heads,
                          n_steps=n_steps),
        grid_spec=pltpu.PrefetchScalarGridSpec(
            num_scalar_prefetch=1,
            grid=(n_steps,),
            in_specs=in_specs,
            out_specs=[
                row_spec,
                pl.BlockSpec(state_block,
                             lambda s, tbl: (field(TBL_STATE)(s, tbl), l, direction, 0, 0, 0)),
            ],
            scratch_shapes=scratch,
        ),
        out_shape=[jax.ShapeDtypeStruct((t, d), BF16 if fused else F32),
                   jax.ShapeDtypeStruct(state_buf.shape, F32)],
        input_output_aliases={len(args) - 1: 1},
        compiler_params=_cparams(1, 40),
    )(*args)


def _ln_swish_store(conv, g_ref, b_ref, o_ref):
    x = conv[...]
    mu = jnp.mean(x, axis=-1, keepdims=True)
    xc = x - mu
    var = jnp.mean(xc * xc, axis=-1, keepdims=True)
    y = xc * lax.rsqrt(var + EPS) * g_ref[...] + b_ref[...]
    o_ref[...] = _silu(y).astype(o_ref.dtype)


def _cfconv_kernel(u_ref, w_ref, b_ref, g_ref, be_ref, o_ref, pad, shifted, conv, *,
                   seq, block, width, k, n_ctx_blocks):
    hk = k // 2
    lead = 16
    span = seq + 24
    rows = block // GRID_W
    i = pl.program_id(0)

    @pl.when(i < n_ctx_blocks)
    def _():
        pad[0:lead, :] = jnp.zeros((lead, width), F32)
        pad[lead + seq:2 * lead + seq, :] = jnp.zeros((lead, width), F32)
        for sq in range(block // seq):
            pad[lead:lead + seq, :] = u_ref[sq * seq:(sq + 1) * seq, :]

            def body(cbi, carry):
                ds = pl.ds(pl.multiple_of(cbi * 128, 128), 128)
                for res in range(1, 8):
                    shifted[res] = pad[pl.ds(res, span), ds]
                acc = jnp.zeros((seq, 128), F32)
                for tap in range(k):
                    base, res = divmod(lead - hk + tap, 8)
                    if res == 0:
                        src = pad[pl.ds(8 * base, seq), ds]
                    else:
                        src = shifted[res, pl.ds(8 * base, seq), :]
                    acc = acc + w_ref[tap:tap + 1, ds] * src
                conv[pl.ds(sq * seq, seq), ds] = acc + b_ref[:, ds]
                return carry

            lax.fori_loop(0, width // 128, body, 0)

    @pl.when(i >= n_ctx_blocks)
    def _():
        def body(cbi, carry):
            ds = pl.ds(pl.multiple_of(cbi * 128, 128), 128)
            for r in range(rows):
                acc = None
                for tap in range(k):
                    src = r + tap - hk
                    if 0 <= src < rows:
                        term = w_ref[tap:tap + 1, ds] * u_ref[pl.ds(src * GRID_W, GRID_W), ds]
                        acc = term if acc is None else acc + term
                conv[pl.ds(r * GRID_W, GRID_W), ds] = acc + b_ref[:, ds]
            return carry

        lax.fori_loop(0, width // 128, body, 0)

    _ln_swish_store(conv, g_ref, be_ref, o_ref)


def _cfconv(u, w, b, g, be, *, t_ctx, seq, dec_seq):
    t, width = u.shape
    k = w.shape[0]
    assert dec_seq % seq == 0 and t_ctx % dec_seq == 0 and k // 2 < 16
    vec = pl.BlockSpec((1, width), lambda i: (0, 0))
    return pl.pallas_call(
        functools.partial(_cfconv_kernel, seq=seq, block=dec_seq, width=width, k=k,
                          n_ctx_blocks=t_ctx // dec_seq),
        grid=(t // dec_seq,),
        in_specs=[pl.BlockSpec((dec_seq, width), lambda i: (i, 0)),
                  pl.BlockSpec((k, width), lambda i: (0, 0)), vec, vec, vec],
        out_specs=pl.BlockSpec((dec_seq, width), lambda i: (i, 0)),
        out_shape=jax.ShapeDtypeStruct((t, width), BF16),
        scratch_shapes=[pltpu.VMEM((seq + 32, width), F32),
                        pltpu.VMEM((8, seq + 24, 128), F32),
                        pltpu.VMEM((dec_seq, width), F32)],
        compiler_params=_cparams(1, 48),
    )(u, w, b, g, be)


def _merge_kernel(y_ref, v_ref, u_ref, g0_ref, g1_ref, g2_ref, w0_ref, w1_ref, w2_ref,
                  o_ref, wb0, wb1, wb2):
    @pl.when(pl.program_id(1) == 0)
    def _():
        wb0[...] = w0_ref[...].astype(BF16)
        wb1[...] = w1_ref[...].astype(BF16)
        wb2[...] = w2_ref[...].astype(BF16)

    o0 = jnp.dot(y_ref[...], wb0[...], preferred_element_type=F32)
    o1 = jnp.dot(v_ref[...], wb1[...], preferred_element_type=F32)
    o2 = jnp.dot(u_ref[...], wb2[...], preferred_element_type=F32)
    merged = (g0_ref[...].astype(F32) * o0 + g1_ref[...].astype(F32) * o1
              + g2_ref[...].astype(F32) * o2)
    o_ref[...] = merged.astype(BF16)


def _merge(y, v, u, gates, ssd_out, sc_out, cf_out, l):
    t, d = y.shape
    kv, ku = v.shape[1], u.shape[1]
    tm, tn = 512, 512
    nj = d // tn
    one = pl.Buffered(1)

    def wspec(k):
        return pl.BlockSpec((None, k, tn), lambda j, i: (l, 0, j), pipeline_mode=one)

    return pl.pallas_call(
        _merge_kernel,
        grid=(nj, t // tm),
        in_specs=[
            pl.BlockSpec((tm, d), lambda j, i: (i, 0)),
            pl.BlockSpec((tm, kv), lambda j, i: (i, 0)),
            pl.BlockSpec((tm, ku), lambda j, i: (i, 0)),
            pl.BlockSpec((tm, tn), lambda j, i: (i, j)),
            pl.BlockSpec((tm, tn), lambda j, i: (i, nj + j)),
            pl.BlockSpec((tm, tn), lambda j, i: (i, 2 * nj + j)),
            wspec(d), wspec(kv), wspec(ku),
        ],
        out_specs=pl.BlockSpec((tm, tn), lambda j, i: (i, j)),
        out_shape=jax.ShapeDtypeStruct((t, d), BF16),
        scratch_shapes=[pltpu.VMEM((d, tn), BF16), pltpu.VMEM((kv, tn), BF16),
                        pltpu.VMEM((ku, tn), BF16)],
        compiler_params=_cparams(2, 48),
    )(y, v, u, gates, gates, gates, ssd_out, sc_out, cf_out)


def _wo_kernel(m_ref, w_ref, xa_ref, xb_ref, g_ref, o_ref, wb, *, tm, geo):
    i = pl.program_id(1)

    @pl.when(i == 0)
    def _():
        wb[...] = w_ref[...].astype(BF16)

    row = _mod_row(i * tm, *geo)
    out = g_ref[pl.ds(row, 1), :] * jnp.dot(m_ref[...], wb[...], preferred_element_type=F32)

    @pl.when(i * tm < geo[0])
    def _():
        o_ref[...] = xa_ref[...] + out

    @pl.when(i * tm >= geo[0])
    def _():
        o_ref[...] = xb_ref[...] + out


def _wo(merged, w_o, x_parts, mod, l, geo):
    t, d = merged.shape
    r = mod.shape[1]
    tm, tn = 512, min(d, 1024)
    nj = d // tn
    return pl.pallas_call(
        functools.partial(_wo_kernel, tm=tm, geo=geo),
        grid=(nj, t // tm),
        in_specs=[
            pl.BlockSpec((tm, d), lambda j, i: (i, 0)),
            pl.BlockSpec((None, d, tn), lambda j, i: (l, 0, j), pipeline_mode=pl.Buffered(1)),
        ] + _split_specs(x_parts, tm, tn, geo[0] // tm, 1, 0) + [
            pl.BlockSpec((None, r, tn), lambda j, i: (l, 0, 2 * nj + j)),
        ],
        out_specs=pl.BlockSpec((tm, tn), lambda j, i: (i, j)),
        out_shape=jax.ShapeDtypeStruct((t, d), F32),
        scratch_shapes=[pltpu.VMEM((d, tn), BF16)],
        compiler_params=_cparams(2, VMEM_BUDGET_MIB),
    )(merged, w_o, *x_parts, mod)


def _router_kernel(x_ref, g_ref, sh_ref, s_ref, rw_ref, rb_ref,
                   hp_ref, idx_ref, wt_ref, rank_ref, cnt_ref, carry, *, tm, geo, n_exp):
    i = pl.program_id(0)

    @pl.when(i == 0)
    def _():
        carry[...] = jnp.zeros(carry.shape, F32)

    row = _mod_row(i * tm, *geo)
    x = x_ref[...]
    y = x * lax.rsqrt(jnp.mean(x * x, axis=-1, keepdims=True) + EPS) * g_ref[...]
    h = y * (1.0 + s_ref[pl.ds(row, 1), :]) + sh_ref[pl.ds(row, 1), :]
    half = h.shape[1] // 2
    lo = lax.bitcast_convert_type(h[:, :half].astype(BF16).astype(F32), U32)
    hi = lax.bitcast_convert_type(h[:, half:].astype(BF16).astype(F32), U32)
    hp_ref[...] = (lo >> 16) | (hi & jnp.uint32(0xFFFF0000))

    logits = _dot_split(h, rw_ref[...])
    lt = logits.T[0:n_exp, :]
    score = _sigmoid(lt)
    sel = score + rb_ref[...]
    srow = [sel[e:e + 1, :] for e in range(n_exp)]
    prow = [score[e:e + 1, :] for e in range(n_exp)]

    def group_score(gi):
        v = srow[gi * GROUP_SIZE:(gi + 1) * GROUP_SIZE]
        best = None
        for a in range(GROUP_SIZE):
            for b in range(a + 1, GROUP_SIZE):
                s = v[a] + v[b]
                best = s if best is None else jnp.maximum(best, s)
        return best

    best_g = group_score(0)
    bi = jnp.zeros(best_g.shape, I32)
    for gi in range(1, N_EXPERT_GROUPS):
        gs = group_score(gi)
        upd = gs > best_g
        best_g = jnp.where(upd, gs, best_g)
        bi = jnp.where(upd, gi, bi)

    def pick(rows, j):
        out = rows[j]
        for gi in range(1, N_EXPERT_GROUPS):
            out = jnp.where(bi == gi, rows[gi * GROUP_SIZE + j], out)
        return out

    v = [pick(srow, j) for j in range(GROUP_SIZE)]
    p = [pick(prow, j) for j in range(GROUP_SIZE)]

    def argmax_first(vals):
        best, arg = vals[0], jnp.zeros(vals[0].shape, I32)
        for j in range(1, GROUP_SIZE):
            upd = vals[j] > best
            best = jnp.where(upd, vals[j], best)
            arg = jnp.where(upd, j, arg)
        return arg

    i1 = argmax_first(v)
    i2 = argmax_first([jnp.where(i1 == j, -jnp.inf, v[j]) for j in range(GROUP_SIZE)])
    p1 = sum(jnp.where(i1 == j, p[j], 0.0) for j in range(GROUP_SIZE))
    p2 = sum(jnp.where(i2 == j, p[j], 0.0) for j in range(GROUP_SIZE))
    den = p1 + p2
    e1 = bi * GROUP_SIZE + i1
    e2 = bi * GROUP_SIZE + i2
    idx_ref[0:1, :] = e1
    idx_ref[1:2, :] = e2
    wt_ref[0:1, :] = p1 / den
    wt_ref[1:2, :] = p2 / den

    erow = lax.broadcasted_iota(I32, (n_exp, tm), 0)
    oh1 = (erow == e1).astype(F32)
    oh2 = (erow == e2).astype(F32)
    cnt = oh1 + oh2
    before = (lax.broadcasted_iota(I32, (tm, tm), 0)
              < lax.broadcasted_iota(I32, (tm, tm), 1)).astype(BF16)
    prior = jnp.dot(cnt.astype(BF16), before, preferred_element_type=F32) + carry[...]
    rank_ref[0:1, :] = jnp.sum(oh1 * prior, axis=0, keepdims=True).astype(I32)
    rank_ref[1:2, :] = jnp.sum(oh2 * prior, axis=0, keepdims=True).astype(I32)
    carry[...] = carry[...] + jnp.sum(cnt, axis=1, keepdims=True)
    cnt_ref[...] = jnp.broadcast_to(carry[...], cnt_ref.shape)


def _router(x, g, mod, l, rw_pad, rb_col, geo):
    t, d = x.shape
    r = mod.shape[1]
    n_exp = rb_col.shape[0]
    tm = 256
    return pl.pallas_call(
        functools.partial(_router_kernel, tm=tm, geo=geo, n_exp=n_exp),
        grid=(t // tm,),
        in_specs=[
            pl.BlockSpec((tm, d), lambda i: (i, 0)),
            pl.BlockSpec((1, d), lambda i: (0, 0)),
            pl.BlockSpec((None, r, d), lambda i: (l, 0, 3)),
            pl.BlockSpec((None, r, d), lambda i: (l, 0, 4)),
            pl.BlockSpec(rw_pad.shape, lambda i: (0, 0)),
            pl.BlockSpec(rb_col.shape, lambda i: (0, 0)),
        ],
        out_specs=[
            pl.BlockSpec((tm, d // 2), lambda i: (i, 0)),
            pl.BlockSpec((2, tm), lambda i: (0, i)),
            pl.BlockSpec((2, tm), lambda i: (0, i)),
            pl.BlockSpec((2, tm), lambda i: (0, i)),
            pl.BlockSpec((n_exp, 128), lambda i: (0, 0)),
        ],
        out_shape=[
            jax.ShapeDtypeStruct((t, d // 2), U32),
            jax.ShapeDtypeStruct((2, t), I32),
            jax.ShapeDtypeStruct((2, t), F32),
            jax.ShapeDtypeStruct((2, t), I32),
            jax.ShapeDtypeStruct((n_exp, 128), F32),
        ],
        scratch_shapes=[pltpu.VMEM((n_exp, 1), F32)],
        compiler_params=_cparams(1, 32),
    )(x, g, mod, mod, rw_pad, rb_col)


def _row_copy(src, dst, sem):
    return pltpu.make_async_copy(src, dst, sem)


def _dispatch_kernel(idx_s, rank_s, off_s, h_ref, xs_in, xs_out, sem, *, tm, t):
    del xs_in
    base = pl.program_id(0) * tm

    def issue(r, carry):
        for k in range(2):
            pos = k * t + base + r
            dst = off_s[idx_s[pos]] + rank_s[pos]
            _row_copy(h_ref.at[pl.ds(r, 1)], xs_out.at[pl.ds(dst, 1)], sem).start()
        return carry

    lax.fori_loop(0, tm, issue, 0, unroll=4)

    def drain(r, carry):
        for k in range(2):
            _row_copy(h_ref.at[pl.ds(0, 1)], xs_out.at[pl.ds(0, 1)], sem).wait()
        return carry

    lax.fori_loop(0, tm, drain, 0, unroll=4)


def _dispatch(hp, idx_flat, rank_flat, off, n_rows):
    t, w = hp.shape
    tm = 256
    xs0 = jnp.zeros((n_rows, w), U32)
    return pl.pallas_call(
        functools.partial(_dispatch_kernel, tm=tm, t=t),
        grid_spec=pltpu.PrefetchScalarGridSpec(
            num_scalar_prefetch=3,
            grid=(t // tm,),
            in_specs=[pl.BlockSpec((tm, w), lambda i, *_: (i, 0)),
                      pl.BlockSpec(memory_space=pl.ANY)],
            out_specs=pl.BlockSpec(memory_space=pl.ANY),
            scratch_shapes=[pltpu.SemaphoreType.DMA(())],
        ),
        out_shape=jax.ShapeDtypeStruct((n_rows, w), U32),
        input_output_aliases={4: 0},
        compiler_params=_cparams(1, 32),
    )(idx_flat, rank_flat, off, hp, xs0)


def _expert_changed(te_s, i):
    return (i == 0) | (te_s[i] != te_s[jnp.maximum(i - 1, 0)])


def _up_kernel(te_s, nu_s, x_ref, wg_ref, wu_ref, o_ref, wgb, wub):
    i = pl.program_id(1)

    @pl.when(_expert_changed(te_s, i))
    def _():
        wgb[...] = wg_ref[...].astype(BF16)
        wub[...] = wu_ref[...].astype(BF16)

    @pl.when(i < nu_s[0])
    def _():
        pk = x_ref[...]
        lo = lax.bitcast_convert_type(pk << 16, F32)
        hi = lax.bitcast_convert_type(pk & jnp.uint32(0xFFFF0000), F32)
        x = jnp.concatenate([lo, hi], axis=1).astype(BF16)
        gt = jnp.dot(x, wgb[...], preferred_element_type=F32)
        up = jnp.dot(x, wub[...], preferred_element_type=F32)
        o_ref[...] = (_silu(gt) * up).astype(BF16)

    @pl.when(i >= nu_s[0])
    def _():
        o_ref[...] = jnp.zeros(o_ref.shape, BF16)


def _moe_up(xs, te, nu, w_gate, w_up, l, tm):
    n_rows, w = xs.shape
    d, f = w_gate.shape[2], w_gate.shape[3]
    tf = 256
    wspec = pl.BlockSpec((None, None, d, tf), lambda j, i, te_s, nu_s: (l, te_s[i], 0, j))
    return pl.pallas_call(
        _up_kernel,
        grid_spec=pltpu.PrefetchScalarGridSpec(
            num_scalar_prefetch=2,
            grid=(f // tf, n_rows // tm),
            in_specs=[pl.BlockSpec((tm, w), lambda j, i, *_: (i, 0)), wspec, wspec],
            out_specs=pl.BlockSpec((tm, tf), lambda j, i, *_: (i, j)),
            scratch_shapes=[pltpu.VMEM((d, tf), BF16)] * 2,
        ),
        out_shape=jax.ShapeDtypeStruct((n_rows, f), BF16),
        compiler_params=_cparams(2, 48),
    )(te, nu, xs, w_gate, w_up)


def _down_kernel(te_s, nu_s, a_ref, w_ref, o_ref, wb):
    i = pl.program_id(1)

    @pl.when(_expert_changed(te_s, i))
    def _():
        wb[...] = w_ref[...].astype(BF16)

    @pl.when(i < nu_s[0])
    def _():
        o_ref[...] = jnp.dot(a_ref[...], wb[...], preferred_element_type=F32)

    @pl.when(i >= nu_s[0])
    def _():
        o_ref[...] = jnp.zeros(o_ref.shape, F32)


def _moe_down(act, te, nu, w_down, l, tm):
    n_rows, f = act.shape
    d = w_down.shape[3]
    tn = min(d, 2048)
    return pl.pallas_call(
        _down_kernel,
        grid_spec=pltpu.PrefetchScalarGridSpec(
            num_scalar_prefetch=2,
            grid=(d // tn, n_rows // tm),
            in_specs=[pl.BlockSpec((tm, f), lambda j, i, *_: (i, 0)),
                      pl.BlockSpec((None, None, f, tn), lambda j, i, te_s, nu_s: (l, te_s[i], 0, j))],
            out_specs=pl.BlockSpec((tm, tn), lambda j, i, *_: (i, j)),
            scratch_shapes=[pltpu.VMEM((f, tn), BF16)],
        ),
        out_shape=jax.ShapeDtypeStruct((n_rows, d), F32),
        compiler_params=_cparams(2, 48),
    )(te, nu, act, w_down)


def _combine_kernel(idx_s, rank_s, off_s, x_ref, wt_ref, g2_ref, ng_ref, sh_ref, s_ref, ys_ref,
                    *rest, tm, t, geo, final):
    o_ref, o2_ref, buf, sem = rest
    i = pl.program_id(0)
    n_steps = t // tm
    slot = lax.rem(i, 2)

    def gather(tile, dst_slot):
        def issue(r, carry):
            for k in range(2):
                pos = k * t + tile * tm + r
                src = off_s[idx_s[pos]] + rank_s[pos]
                _row_copy(ys_ref.at[pl.ds(src, 1)], buf.at[dst_slot, k, pl.ds(r, 1)],
                          sem.at[dst_slot]).start()
            return carry

        lax.fori_loop(0, tm, issue, 0, unroll=4)

    @pl.when(i == 0)
    def _():
        gather(0, 0)

    @pl.when(i + 1 < n_steps)
    def _():
        gather(i + 1, 1 - slot)

    def drain(r, carry):
        for k in range(2):
            _row_copy(ys_ref.at[pl.ds(0, 1)], buf.at[slot, k, pl.ds(0, 1)], sem.at[slot]).wait()
        return carry

    lax.fori_loop(0, tm, drain, 0, unroll=4)

    row = _mod_row(i * tm, *geo)
    wt = wt_ref[...]
    moe = wt[:, 0:1] * buf[slot, 0] + wt[:, 1:2] * buf[slot, 1]
    xn = x_ref[...] + g2_ref[pl.ds(row, 1), :] * moe
    y = xn * lax.rsqrt(jnp.mean(xn * xn, axis=-1, keepdims=True) + EPS) * ng_ref[...]
    if final:
        @pl.when(i * tm < geo[0])
        def _():
            o_ref[...] = y

        @pl.when(i * tm >= geo[0])
        def _():
            o2_ref[...] = y
    else:
        o_ref[...] = xn
        o2_ref[...] = (y * (1.0 + s_ref[pl.ds(row, 1), :]) + sh_ref[pl.ds(row, 1), :]).astype(BF16)


def _combine(x, wt_col, mod, l, norm_g, ys, idx_flat, rank_flat, off, geo, final):
    t, d = x.shape
    r = mod.shape[1]
    tm = 256
    l_next = l if final else l + 1
    if final:
        n_ctx_tiles = geo[0] // tm
        out_specs = [pl.BlockSpec((tm, d), lambda i, *_: (jnp.minimum(i, n_ctx_tiles - 1), 0)),
                     pl.BlockSpec((tm, d), lambda i, *_: (jnp.maximum(i - n_ctx_tiles, 0), 0))]
        out_shape = [jax.ShapeDtypeStruct((geo[0], d), F32),
                     jax.ShapeDtypeStruct((t - geo[0], d), F32)]
    else:
        out_specs = [pl.BlockSpec((tm, d), lambda i, *_: (i, 0))] * 2
        out_shape = [jax.ShapeDtypeStruct((t, d), F32), jax.ShapeDtypeStruct((t, d), BF16)]
    return pl.pallas_call(
        functools.partial(_combine_kernel, tm=tm, t=t, geo=geo, final=final),
        grid_spec=pltpu.PrefetchScalarGridSpec(
            num_scalar_prefetch=3,
            grid=(t // tm,),
            in_specs=[
                pl.BlockSpec((tm, d), lambda i, *_: (i, 0)),
                pl.BlockSpec((tm, 2), lambda i, *_: (i, 0)),
                pl.BlockSpec((None, r, d), lambda i, *_: (l, 0, 5)),
                pl.BlockSpec((1, d), lambda i, *_: (0, 0)),
                pl.BlockSpec((None, r, d), lambda i, *_: (l_next, 0, 0)),
                pl.BlockSpec((None, r, d), lambda i, *_: (l_next, 0, 1)),
                pl.BlockSpec(memory_space=pl.ANY),
            ],
            out_specs=out_specs,
            scratch_shapes=[pltpu.VMEM((2, 2, tm, d), F32), pltpu.SemaphoreType.DMA((2,))],
        ),
        out_shape=out_shape,
        compiler_params=_cparams(1, 48),
    )(idx_flat, rank_flat, off, x, wt_col, mod, norm_g, mod, mod, ys)


def kernel(x_prompt, x_sample, state_ssd, c, c_ctx, ada_w, ada_b, norm1_g, norm2_g, w_in,
           ssd_conv_w, ssd_conv_b, ssd_dt_bias, ssd_a_log, ssd_d, ssd_norm_g, ssd_out,
           sc_conv_w, sc_conv_b, sc_out, cf_conv_w, cf_conv_b, cf_ln_g, cf_ln_b, cf_out, w_o,
           router_w, router_bias, moe_w_gate, moe_w_up, moe_w_down, final_g):
    n_ctx, seq, d = x_prompt.shape
    n_dec, dec_seq, _ = x_sample.shape
    depth = w_in.shape[0]
    n_heads = d // HEAD_DIM
    gn = SSD_GROUPS * SSD_STATE
    xbc_w = d + 2 * gn
    sc_w = sc_out.shape[1]
    cf_w = cf_out.shape[1]
    n_exp = router_w.shape[1]
    t_ctx = n_ctx * seq
    t = t_ctx + n_dec * dec_seq
    geo = (t_ctx, dec_seq, n_dec)
    off_xbc = d
    off_dt = off_xbc + xbc_w
    off_sc = off_dt + 2 * n_heads
    off_cf = off_sc + 3 * sc_w
    off_gate = off_cf + 2 * cf_w

    x_parts = (x_prompt.reshape(t_ctx, d), x_sample.reshape(n_dec * dec_seq, d))
    n_mod_rows = -(-(n_dec + 1) // 8) * 8
    cond = jnp.concatenate([c, c_ctx[None, :],
                            jnp.zeros((n_mod_rows - n_dec - 1, d), F32)], axis=0)
    mod = _ada_table(cond, ada_w, ada_b)

    h0_lat = state_ssd.reshape(n_dec, depth, 2, n_heads // 2, PAIR, SSD_STATE)
    state_buf = jnp.zeros((n_ctx, depth, 2, n_heads // 2, PAIR, SSD_STATE), F32)
    rw_pad = jnp.pad(router_w, ((0, 0), (0, 128 - n_exp)))
    rb_col = router_bias.reshape(n_exp, 1)
    tm_e = 512
    n_rows = 2 * t + n_exp * tm_e
    n_tiles = n_rows // tm_e

    h = _normmod(x_parts, norm1_g[0:1], mod, 0, geo)
    y_final = None
    for l in range(depth):
        z = _proj(h, w_in, l, [0], d, 512, 1024, _epi_identity, [], [], F32)
        xbc = _proj(
            h, w_in, l, [off_xbc], xbc_w, 512, dec_seq,
            functools.partial(_epi_xbc, tm=dec_seq, t_ctx=t_ctx, seq=seq, dec_seq=dec_seq),
            [ssd_conv_w, ssd_conv_b.reshape(depth, 1, xbc_w)],
            [pl.BlockSpec((None, 3, 512), lambda j, i: (l, 0, j)),
             pl.BlockSpec((None, 1, 512), lambda j, i: (l, 0, j))], F32)
        dt = _proj(h, w_in, l, [off_dt], 2 * n_heads, 2 * n_heads, 1024, _epi_dt,
                   [ssd_dt_bias.reshape(depth, 1, 2 * n_heads)],
                   [pl.BlockSpec((None, 1, 2 * n_heads), lambda j, i: (l, 0, 0))], F32)
        v = _proj(
            h, w_in, l, [off_sc, off_sc + sc_w, off_sc + 2 * sc_w], sc_w, 256, 512,
            functools.partial(_epi_sc, tm=512, t_ctx=t_ctx, seq=seq),
            [sc_conv_w, sc_conv_b.reshape(depth, 1, sc_w)],
            [pl.BlockSpec((None, 3, 256), lambda j, i: (l, 0, j)),
             pl.BlockSpec((None, 1, 256), lambda j, i: (l, 0, j))], BF16)
        u_cf = _proj(h, w_in, l, [off_cf, off_cf + cf_w], cf_w, 256, 1024, _epi_cf, [], [], F32)
        gates = _proj(h, w_in, l, [off_gate], 3 * d, 512, 1024, _epi_sigmoid, [], [], BF16)

        alog = ssd_a_log[l].reshape(1, 2 * n_heads)
        dskip = jnp.repeat(ssd_d[l], HEAD_DIM).reshape(1, d)
        ng = ssd_norm_g[l:l + 1]
        scan = dict(d=d, l=l, shape=(n_ctx, seq, n_dec, dec_seq))
        yf, state_buf = _ssd_pass(xbc, dt, alog, h0_lat, state_buf, reverse=False, **scan)
        y_ssd, state_buf = _ssd_pass(xbc, dt, alog, h0_lat, state_buf, reverse=True,
                                     fused_in=(yf, z, dskip, ng), **scan)

        u = _cfconv(u_cf, cf_conv_w[l], cf_conv_b[l:l + 1], cf_ln_g[l:l + 1], cf_ln_b[l:l + 1],
                    t_ctx=t_ctx, seq=seq, dec_seq=dec_seq)

        merged = _merge(y_ssd, v, u, gates, ssd_out, sc_out, cf_out, l)
        x = _wo(merged, w_o, x_parts, mod, l, geo)

        hp, idx, wt, rank, cnt = _router(x, norm2_g[l:l + 1], mod, l, rw_pad, rb_col, geo)
        counts = cnt[:, 0].astype(I32)
        tiles = (counts + tm_e - 1) // tm_e
        tile_end = jnp.cumsum(tiles)
        off = (tile_end - tiles) * tm_e
        n_used = tile_end[-1]
        tile_id = jnp.minimum(jnp.arange(n_tiles, dtype=I32), n_used - 1)
        te = jnp.sum(tile_id[:, None] >= tile_end[None, :], axis=1).astype(I32)
        nu = n_used.reshape(1).astype(I32)
        idx_flat = idx.reshape(2 * t)
        rank_flat = rank.reshape(2 * t)
        xs = _dispatch(hp, idx_flat, rank_flat, off, n_rows)
        act = _moe_up(xs, te, nu, moe_w_gate, moe_w_up, l, tm_e)
        ys = _moe_down(act, te, nu, moe_w_down, l, tm_e)
        final = l == depth - 1
        ng_next = final_g.reshape(1, d) if final else norm1_g[l + 1:l + 2]
        outs = _combine(x, wt.T, mod, l, ng_next, ys, idx_flat, rank_flat, off, geo, final)
        if final:
            y_final = outs
        else:
            x, h = outs
            x_parts = (x, x)

    y_prompt = y_final[0].reshape(n_ctx, seq, d)
    y_sample = y_final[1].reshape(n_dec, dec_seq, d)
    new_state = state_buf.reshape(n_ctx, depth, 2, n_heads, HEAD_DIM, SSD_STATE)
    return (y_prompt, y_sample, new_state)
```

```python
import functools

import jax
import jax.numpy as jnp
import numpy as np
from jax import lax
from jax.experimental import pallas as pl
from jax.experimental.pallas import tpu as pltpu

F32 = jnp.float32
BF16 = jnp.bfloat16
I32 = jnp.int32
U32 = jnp.uint32

EPS = 1e-6
GRID_W = 64
HEAD_DIM = 64
PAIR = 2 * HEAD_DIM
SSD_GROUPS = 8
SSD_STATE = 128
CHUNK = 128
N_EXPERT_GROUPS = 4
GROUP_SIZE = 4
MIB = 1024 * 1024
VMEM_BUDGET_MIB = 58


def _cparams(n_axes, vmem_mib):
    return pltpu.CompilerParams(
        dimension_semantics=("arbitrary",) * n_axes,
        vmem_limit_bytes=min(vmem_mib, VMEM_BUDGET_MIB) * MIB)


def _sigmoid(x):
    return 1.0 / (1.0 + jnp.exp(-x))


def _silu(x):
    return x * _sigmoid(x)


def _mod_row(start, t_ctx, dec_seq, n_dec):
    return jnp.where(start < t_ctx, n_dec, lax.div(start - t_ctx, dec_seq))


def _dot_split(a, b):
    a_hi, b_hi = a.astype(BF16), b.astype(BF16)
    a_lo = (a - a_hi.astype(F32)).astype(BF16)
    b_lo = (b - b_hi.astype(F32)).astype(BF16)
    return (jnp.dot(a_hi, b_hi, preferred_element_type=F32)
            + jnp.dot(a_lo, b_hi, preferred_element_type=F32)
            + jnp.dot(a_hi, b_lo, preferred_element_type=F32))


def _ada_kernel(c_ref, w_ref, b_ref, o_ref):
    o_ref[...] = _dot_split(_silu(c_ref[...]), w_ref[...]) + b_ref[...]


def _ada_table(cond, ada_w, ada_b):
    depth, d, n = ada_w.shape
    r = cond.shape[0]
    tn = 1024
    return pl.pallas_call(
        _ada_kernel,
        grid=(depth, n // tn),
        in_specs=[
            pl.BlockSpec((r, d), lambda l, j: (0, 0)),
            pl.BlockSpec((None, d, tn), lambda l, j: (l, 0, j)),
            pl.BlockSpec((None, 1, tn), lambda l, j: (l, 0, j)),
        ],
        out_specs=pl.BlockSpec((None, r, tn), lambda l, j: (l, 0, j)),
        out_shape=jax.ShapeDtypeStruct((depth, r, n), F32),
        compiler_params=_cparams(2, 56),
    )(cond, ada_w, ada_b.reshape(depth, 1, n))


def _split_specs(x_parts, tm, width, n_ctx_tiles, tile_axis, col_axis=None):
    xa, xb = x_parts
    b_off = n_ctx_tiles if xa is xb else 0

    def col(ids):
        return 0 if col_axis is None else ids[col_axis]

    def amap(*ids):
        return (jnp.minimum(ids[tile_axis], n_ctx_tiles - 1), col(ids))

    def bmap(*ids):
        return (b_off + jnp.maximum(ids[tile_axis] - n_ctx_tiles, 0), col(ids))

    return [pl.BlockSpec((tm, width), amap), pl.BlockSpec((tm, width), bmap)]


def _normmod_kernel(xa_ref, xb_ref, g_ref, sh_ref, s_ref, o_ref, *, tm, geo):
    start = pl.program_id(0) * tm
    row = _mod_row(start, *geo)

    def emit(x_ref):
        x = x_ref[...]
        y = x * lax.rsqrt(jnp.mean(x * x, axis=-1, keepdims=True) + EPS) * g_ref[...]
        o_ref[...] = (y * (1.0 + s_ref[pl.ds(row, 1), :]) + sh_ref[pl.ds(row, 1), :]).astype(BF16)

    pl.when(start < geo[0])(lambda: emit(xa_ref))
    pl.when(start >= geo[0])(lambda: emit(xb_ref))


def _normmod(x_parts, g, mod, l, geo):
    d = x_parts[0].shape[1]
    t = geo[0] + geo[1] * geo[2]
    r = mod.shape[1]
    tm = 256
    return pl.pallas_call(
        functools.partial(_normmod_kernel, tm=tm, geo=geo),
        grid=(t // tm,),
        in_specs=_split_specs(x_parts, tm, d, geo[0] // tm, 0) + [
            pl.BlockSpec((1, d), lambda i: (0, 0)),
            pl.BlockSpec((None, r, d), lambda i: (l, 0, 0)),
            pl.BlockSpec((None, r, d), lambda i: (l, 0, 1)),
        ],
        out_specs=pl.BlockSpec((tm, d), lambda i: (i, 0)),
        out_shape=jax.ShapeDtypeStruct((t, d), BF16),
        compiler_params=_cparams(1, 32),
    )(*x_parts, g, mod, mod)


def _proj_kernel(*refs, n_dots, n_extra, epilogue):
    h_ref = refs[0]
    w_refs = refs[1:1 + n_dots]
    extra = refs[1 + n_dots:1 + n_dots + n_extra]
    o_ref = refs[1 + n_dots + n_extra]
    wb = refs[2 + n_dots + n_extra:]
    i = pl.program_id(1)

    @pl.when(i == 0)
    def _():
        for d in range(n_dots):
            wb[d][...] = w_refs[d][...].astype(BF16)

    h = h_ref[...]
    accs = [jnp.dot(h, wb[d][...], preferred_element_type=F32) for d in range(n_dots)]
    o_ref[...] = epilogue(accs, extra, i).astype(o_ref.dtype)


def _proj(h, w_in, l, col_offs, n_cols, tn, tm, epilogue, extra, extra_specs, out_dtype):
    t, k = h.shape
    n_dots = len(col_offs)

    def wspec(off):
        return pl.BlockSpec((pl.Squeezed(), pl.Element(k), pl.Element(tn)),
                            lambda j, i: (l, 0, pl.multiple_of(off + j * tn, 128)))

    out_bytes = jnp.dtype(out_dtype).itemsize
    vmem = (2 * tm * k * 2 + n_dots * k * tn * (2 * 4 + 2) + 2 * tm * tn * out_bytes
            + 3 * n_dots * tm * tn * 4) // MIB + 4
    return pl.pallas_call(
        functools.partial(_proj_kernel, n_dots=n_dots, n_extra=len(extra), epilogue=epilogue),
        grid=(n_cols // tn, t // tm),
        in_specs=[pl.BlockSpec((tm, k), lambda j, i: (i, 0))]
        + [wspec(off) for off in col_offs] + list(extra_specs),
        out_specs=pl.BlockSpec((tm, tn), lambda j, i: (i, j)),
        out_shape=jax.ShapeDtypeStruct((t, n_cols), out_dtype),
        scratch_shapes=[pltpu.VMEM((k, tn), BF16)] * n_dots,
        compiler_params=_cparams(2, vmem),
    )(h, *([w_in] * n_dots), *extra)


def _conv3_rows(u, w_ref, b_ref, period):
    rows = u.shape[0]
    pos = lax.broadcasted_iota(I32, u.shape, 0) & (period - 1)
    prev = jnp.where(pos == 0, 0.0, pltpu.roll(u, 1, 0))
    nxt = jnp.where(pos == period - 1, 0.0, pltpu.roll(u, rows - 1, 0))
    return w_ref[0:1, :] * prev + w_ref[1:2, :] * u + w_ref[2:3, :] * nxt + b_ref[...]


def _epi_identity(accs, extra, i):
    return accs[0]


def _epi_sigmoid(accs, extra, i):
    return _sigmoid(accs[0])


def _epi_xbc(accs, extra, i, *, tm, t_ctx, seq, dec_seq):
    period = jnp.where(i * tm < t_ctx, seq, dec_seq)
    return _silu(_conv3_rows(accs[0], extra[0], extra[1], period))


def _epi_dt(accs, extra, i):
    x = accs[0] + extra[0][...]
    return jnp.maximum(x, 0.0) + jnp.log1p(jnp.exp(-jnp.abs(x)))


def _epi_sc(accs, extra, i, *, tm, t_ctx, seq):
    bg, cg, hv = accs
    period = jnp.where(i * tm < t_ctx, seq, GRID_W)
    return bg * _conv3_rows(cg * hv, extra[0], extra[1], period)


def _epi_cf(accs, extra, i):
    return accs[0] * _sigmoid(accs[1])


TBL_BLOCK, TBL_FIRST, TBL_LAST, TBL_LATENT, TBL_H0, TBL_STATE, TBL_FIELDS = 0, 1, 2, 3, 4, 5, 6


def _ssd_kernel(tbl, *refs, reverse, fused, n_heads, n_steps):
    it = iter(refs)
    xs_ref, b_ref, c_ref, dt_ref, alog_ref, expand_ref, h0_ref = (next(it) for _ in range(7))
    if fused:
        yf_ref, z_ref, dskip_ref, ng_ref = (next(it) for _ in range(4))
    next(it)
    y_ref, st_ref, h_scr = next(it), next(it), next(it)
    ybuf = next(it) if fused else None

    step = pl.program_id(0)
    first = tbl[TBL_FIRST * n_steps + step] == 1
    latent = tbl[TBL_LATENT * n_steps + step] == 1

    @pl.when(first & latent)
    def _():
        h_scr[...] = h0_ref[...]

    @pl.when(first & jnp.logical_not(latent))
    def _():
        h_scr[...] = jnp.zeros(h_scr.shape, F32)

    q = CHUNK
    dt = dt_ref[...]
    dta = dt * (-jnp.exp(alog_ref[...]))
    r_i = lax.broadcasted_iota(I32, (q, q), 0)
    c_i = lax.broadcasted_iota(I32, (q, q), 1)
    tri = (c_i >= r_i) if reverse else (c_i <= r_i)
    d_hi = dta.astype(BF16)
    rem = dta - d_hi.astype(F32)
    d_mid = rem.astype(BF16)
    d_lo = (rem - d_mid.astype(F32)).astype(BF16)
    parts = jnp.dot(tri.astype(BF16), jnp.concatenate([d_hi, d_mid, d_lo], axis=1),
                    preferred_element_type=F32)
    nh2 = dta.shape[1]
    acum = parts[:, 0:nh2] + parts[:, nh2:2 * nh2] + parts[:, 2 * nh2:3 * nh2]
    acum_t = acum.T
    last = acum[0:1, :] if reverse else acum[q - 1:q, :]
    expa = jnp.exp(acum)
    dend = jnp.exp(last - acum)
    cdec = jnp.exp(last)
    lane_lo = lax.broadcasted_iota(I32, (q, PAIR), 1) < HEAD_DIM
    row_lo = lax.broadcasted_iota(I32, (PAIR, SSD_STATE), 0) < HEAD_DIM
    d0 = n_heads if reverse else 0
    hg = n_heads // SSD_GROUPS
    nt = (((1,), (1,)), ((), ()))

    per_head = jnp.concatenate([dt, dt * dend, expa], axis=0)
    hi = per_head.astype(BF16)
    lo = (per_head - hi.astype(F32)).astype(BF16)
    on_lanes = jnp.dot(jnp.concatenate([hi, lo], axis=1), expand_ref[...],
                       preferred_element_type=F32)
    xs_all = xs_ref[...]
    xdt_all = xs_all * on_lanes[0:q]
    xw_all = xs_all * on_lanes[q:2 * q]
    expa_all = on_lanes[2 * q:3 * q]

    pairs_g = hg // 2
    for g in range(SSD_GROUPS):
        cg = c_ref[:, g * SSD_STATE:(g + 1) * SSD_STATE].astype(BF16)
        bg = b_ref[:, g * SSD_STATE:(g + 1) * SSD_STATE].astype(BF16)
        cb = lax.dot_general(cg, bg, nt, preferred_element_type=F32)
        p0 = g * pairs_g
        gslab = slice(p0 * PAIR, (p0 + pairs_g) * PAIR)
        hgrp = jnp.concatenate([h_scr[p0 + jp] for jp in range(pairs_g)], axis=0)
        y_in = lax.dot_general(cg, hgrp.astype(BF16), nt,
                               preferred_element_type=F32) * expa_all[:, gslab]
        st = jnp.dot(xw_all[:, gslab].T.astype(BF16), bg, preferred_element_type=F32)
        for jp in range(pairs_g):
            pair = p0 + jp
            hx = d0 + 2 * pair
            rows = slice(jp * PAIR, (jp + 1) * PAIR)
            keep = jnp.where(row_lo, cdec[:, hx:hx + 1], cdec[:, hx + 1:hx + 2])
            h_scr[pair] = keep * hgrp[rows] + st[rows]

            def decay_w(hh):
                diff = acum[:, hh:hh + 1] - acum_t[hh:hh + 1, :]
                return (cb * jnp.exp(jnp.where(tri, diff, -jnp.inf))).astype(BF16)

            slab = slice(pair * PAIR, (pair + 1) * PAIR)
            w = jnp.concatenate([decay_w(hx), decay_w(hx + 1)], axis=1)
            xdt = xdt_all[:, slab]
            xbd = jnp.concatenate([jnp.where(lane_lo, xdt, 0.0),
                                   jnp.where(lane_lo, 0.0, xdt)], axis=0).astype(BF16)
            y = jnp.dot(w, xbd, preferred_element_type=F32) + y_in[:, rows]
            if fused:
                ybuf[:, slab] = y
            else:
                y_ref[:, slab] = y

    @pl.when((tbl[TBL_LAST * n_steps + step] == 1) & jnp.logical_not(latent))
    def _():
        st_ref[...] = h_scr[...]

    if fused:
        yt = ybuf[...] + yf_ref[...] + dskip_ref[...] * xs_all
        yt = yt * _silu(z_ref[...])
        yn = yt * lax.rsqrt(jnp.mean(yt * yt, axis=-1, keepdims=True) + EPS) * ng_ref[...]
        y_ref[...] = yn.astype(BF16)


def _ssd_table(n_ctx, seq, n_dec, dec_seq, reverse):
    steps = []
    blk = 0
    for n_seq, length, latent in ((n_ctx, seq, 0), (n_dec, dec_seq, 1)):
        n_chunks = length // CHUNK
        for b in range(n_seq):
            for c in range(n_chunks):
                steps.append((blk + c, int(c == 0), int(c == n_chunks - 1), latent,
                              b if latent else 0, n_ctx - 1 if latent else b))
            blk += n_chunks
    if reverse:
        steps = [(s[0], s[2], s[1], s[3], s[4], s[5]) for s in reversed(steps)]
    return np.asarray(steps, np.int32).T.reshape(-1), len(steps)


def _ssd_pass(xbc, dt, alog, h0, state_buf, *, d, l, shape, reverse, fused_in=None):
    n_heads = d // HEAD_DIM
    gn = SSD_GROUPS * SSD_STATE
    direction = 1 if reverse else 0
    fused = fused_in is not None
    t = xbc.shape[0]
    table, n_steps = _ssd_table(*shape, reverse)

    def field(f):
        return lambda s, tbl: tbl[f * n_steps + s]

    blk = field(TBL_BLOCK)
    row_spec = pl.BlockSpec((CHUNK, d), lambda s, tbl: (blk(s, tbl), 0))
    vec_spec = pl.BlockSpec((1, d), lambda s, tbl: (0, 0))
    state_block = (None, None, None, n_heads // 2, PAIR, SSD_STATE)
    in_specs = [
        row_spec,
        pl.BlockSpec((CHUNK, gn), lambda s, tbl: (blk(s, tbl), d // gn)),
        pl.BlockSpec((CHUNK, gn), lambda s, tbl: (blk(s, tbl), d // gn + 1)),
        pl.BlockSpec((CHUNK, 2 * n_heads), lambda s, tbl: (blk(s, tbl), 0)),
        pl.BlockSpec((1, 2 * n_heads), lambda s, tbl: (0, 0)),
        pl.BlockSpec((4 * n_heads, d), lambda s, tbl: (0, 0)),
        pl.BlockSpec(state_block,
                     lambda s, tbl: (field(TBL_H0)(s, tbl), l, direction, 0, 0, 0)),
    ]
    expand = np.zeros((2 * n_heads, d), np.float32)
    expand[direction * n_heads + np.arange(d) // HEAD_DIM, np.arange(d)] = 1.0
    expand = np.concatenate([expand, expand], axis=0)
    args = [jnp.asarray(table), xbc, xbc, xbc, dt, alog, jnp.asarray(expand, BF16), h0]
    if fused:
        in_specs += [row_spec, row_spec, vec_spec, vec_spec]
        args += list(fused_in)
    in_specs.append(pl.BlockSpec(memory_space=pl.ANY))
    args.append(state_buf)
    scratch = [pltpu.VMEM((n_heads // 2, PAIR, SSD_STATE), F32)]
    if fused:
        scratch.append(pltpu.VMEM((CHUNK, d), F32))
    return pl.pallas_call(
        functools.partial(_ssd_kernel, reverse=reverse, fused=fused, n_heads=n_heads,
                          n_steps=n_steps),
        grid_spec=pltpu.PrefetchScalarGridSpec(
            num_scalar_prefetch=1,
            grid=(n_steps,),
            in_specs=in_specs,
            out_specs=[
                row_spec,
                pl.BlockSpec(state_block,
                             lambda s, tbl: (field(TBL_STATE)(s, tbl), l, direction, 0, 0, 0)),
            ],
            scratch_shapes=scratch,
        ),
        out_shape=[jax.ShapeDtypeStruct((t, d), BF16 if fused else F32),
                   jax.ShapeDtypeStruct(state_buf.shape, F32)],
        input_output_aliases={len(args) - 1: 1},
        compiler_params=_cparams(1, 40),
    )(*args)


def _ln_swish_store(conv, g_ref, b_ref, o_ref):
    x = conv[...]
    mu = jnp.mean(x, axis=-1, keepdims=True)
    xc = x - mu
    var = jnp.mean(xc * xc, axis=-1, keepdims=True)
    y = xc * lax.rsqrt(var + EPS) * g_ref[...] + b_ref[...]
    o_ref[...] = _silu(y).astype(o_ref.dtype)


def _cfconv_kernel(u_ref, w_ref, b_ref, g_ref, be_ref, o_ref, pad, shifted, conv, *,
                   seq, block, width, k, n_ctx_blocks):
    hk = k // 2
    lead = 16
    span = seq + 24
    rows = block // GRID_W
    i = pl.program_id(0)

    @pl.when(i < n_ctx_blocks)
    def _():
        pad[0:lead, :] = jnp.zeros((lead, width), F32)
        pad[lead + seq:2 * lead + seq, :] = jnp.zeros((lead, width), F32)
        for sq in range(block // seq):
            pad[lead:lead + seq, :] = u_ref[sq * seq:(sq + 1) * seq, :]

            def body(cbi, carry):
                ds = pl.ds(pl.multiple_of(cbi * 128, 128), 128)
                for res in range(1, 8):
                    shifted[res] = pad[pl.ds(res, span), ds]
                acc = jnp.zeros((seq, 128), F32)
                for tap in range(k):
                    base, res = divmod(lead - hk + tap, 8)
                    if res == 0:
                        src = pad[pl.ds(8 * base, seq), ds]
                    else:
                        src = shifted[res, pl.ds(8 * base, seq), :]
                    acc = acc + w_ref[tap:tap + 1, ds] * src
                conv[pl.ds(sq * seq, seq), ds] = acc + b_ref[:, ds]
                return carry

            lax.fori_loop(0, width // 128, body, 0)

    @pl.when(i >= n_ctx_blocks)
    def _():
        def body(cbi, carry):
            ds = pl.ds(pl.multiple_of(cbi * 128, 128), 128)
            for r in range(rows):
                acc = None
                for tap in range(k):
                    src = r + tap - hk
                    if 0 <= src < rows:
                        term = w_ref[tap:tap + 1, ds] * u_ref[pl.ds(src * GRID_W, GRID_W), ds]
                        acc = term if acc is None else acc + term
                conv[pl.ds(r * GRID_W, GRID_W), ds] = acc + b_ref[:, ds]
            return carry

        lax.fori_loop(0, width // 128, body, 0)

    _ln_swish_store(conv, g_ref, be_ref, o_ref)


def _cfconv(u, w, b, g, be, *, t_ctx, seq, dec_seq):
    t, width = u.shape
    k = w.shape[0]
    assert dec_seq % seq == 0 and t_ctx % dec_seq == 0 and k // 2 < 16
    vec = pl.BlockSpec((1, width), lambda i: (0, 0))
    return pl.pallas_call(
        functools.partial(_cfconv_kernel, seq=seq, block=dec_seq, width=width, k=k,
                          n_ctx_blocks=t_ctx // dec_seq),
        grid=(t // dec_seq,),
        in_specs=[pl.BlockSpec((dec_seq, width), lambda i: (i, 0)),
                  pl.BlockSpec((k, width), lambda i: (0, 0)), vec, vec, vec],
        out_specs=pl.BlockSpec((dec_seq, width), lambda i: (i, 0)),
        out_shape=jax.ShapeDtypeStruct((t, width), BF16),
        scratch_shapes=[pltpu.VMEM((seq + 32, width), F32),
                        pltpu.VMEM((8, seq + 24, 128), F32),
                        pltpu.VMEM((dec_seq, width), F32)],
        compiler_params=_cparams(1, 48),
    )(u, w, b, g, be)


def _merge_kernel(y_ref, v_ref, u_ref, g0_ref, g1_ref, g2_ref, w0_ref, w1_ref, w2_ref,
                  o_ref, wb0, wb1, wb2):
    @pl.when(pl.program_id(1) == 0)
    def _():
        wb0[...] = w0_ref[...].astype(BF16)
        wb1[...] = w1_ref[...].astype(BF16)
        wb2[...] = w2_ref[...].astype(BF16)

    o0 = jnp.dot(y_ref[...], wb0[...], preferred_element_type=F32)
    o1 = jnp.dot(v_ref[...], wb1[...], preferred_element_type=F32)
    o2 = jnp.dot(u_ref[...], wb2[...], preferred_element_type=F32)
    merged = (g0_ref[...].astype(F32) * o0 + g1_ref[...].astype(F32) * o1
              + g2_ref[...].astype(F32) * o2)
    o_ref[...] = merged.astype(BF16)


def _merge(y, v, u, gates, ssd_out, sc_out, cf_out, l):
    t, d = y.shape
    kv, ku = v.shape[1], u.shape[1]
    tm, tn = 512, 512
    nj = d // tn
    one = pl.Buffered(1)

    def wspec(k):
        return pl.BlockSpec((None, k, tn), lambda j, i: (l, 0, j), pipeline_mode=one)

    return pl.pallas_call(
        _merge_kernel,
        grid=(nj, t // tm),
        in_specs=[
            pl.BlockSpec((tm, d), lambda j, i: (i, 0)),
            pl.BlockSpec((tm, kv), lambda j, i: (i, 0)),
            pl.BlockSpec((tm, ku), lambda j, i: (i, 0)),
            pl.BlockSpec((tm, tn), lambda j, i: (i, j)),
            pl.BlockSpec((tm, tn), lambda j, i: (i, nj + j)),
            pl.BlockSpec((tm, tn), lambda j, i: (i, 2 * nj + j)),
            wspec(d), wspec(kv), wspec(ku),
        ],
        out_specs=pl.BlockSpec((tm, tn), lambda j, i: (i, j)),
        out_shape=jax.ShapeDtypeStruct((t, d), BF16),
        scratch_shapes=[pltpu.VMEM((d, tn), BF16), pltpu.VMEM((kv, tn), BF16),
                        pltpu.VMEM((ku, tn), BF16)],
        compiler_params=_cparams(2, 48),
    )(y, v, u, gates, gates, gates, ssd_out, sc_out, cf_out)


def _wo_kernel(m_ref, w_ref, xa_ref, xb_ref, g_ref, o_ref, wb, *, tm, geo):
    i = pl.program_id(1)

    @pl.when(i == 0)
    def _():
        wb[...] = w_ref[...].astype(BF16)

    row = _mod_row(i * tm, *geo)
    out = g_ref[pl.ds(row, 1), :] * jnp.dot(m_ref[...], wb[...], preferred_element_type=F32)

    @pl.when(i * tm < geo[0])
    def _():
        o_ref[...] = xa_ref[...] + out

    @pl.when(i * tm >= geo[0])
    def _():
        o_ref[...] = xb_ref[...] + out


def _wo(merged, w_o, x_parts, mod, l, geo):
    t, d = merged.shape
    r = mod.shape[1]
    tm, tn = 512, min(d, 1024)
    nj = d // tn
    return pl.pallas_call(
        functools.partial(_wo_kernel, tm=tm, geo=geo),
        grid=(nj, t // tm),
        in_specs=[
            pl.BlockSpec((tm, d), lambda j, i: (i, 0)),
            pl.BlockSpec((None, d, tn), lambda j, i: (l, 0, j), pipeline_mode=pl.Buffered(1)),
        ] + _split_specs(x_parts, tm, tn, geo[0] // tm, 1, 0) + [
            pl.BlockSpec((None, r, tn), lambda j, i: (l, 0, 2 * nj + j)),
        ],
        out_specs=pl.BlockSpec((tm, tn), lambda j, i: (i, j)),
        out_shape=jax.ShapeDtypeStruct((t, d), F32),
        scratch_shapes=[pltpu.VMEM((d, tn), BF16)],
        compiler_params=_cparams(2, VMEM_BUDGET_MIB),
    )(merged, w_o, *x_parts, mod)


def _router_kernel(x_ref, g_ref, sh_ref, s_ref, rw_ref, rb_ref,
                   hp_ref, idx_ref, wt_ref, rank_ref, cnt_ref, carry, *, tm, geo, n_exp):
    i = pl.program_id(0)

    @pl.when(i == 0)
    def _():
        carry[...] = jnp.zeros(carry.shape, F32)

    row = _mod_row(i * tm, *geo)
    x = x_ref[...]
    y = x * lax.rsqrt(jnp.mean(x * x, axis=-1, keepdims=True) + EPS) * g_ref[...]
    h = y * (1.0 + s_ref[pl.ds(row, 1), :]) + sh_ref[pl.ds(row, 1), :]
    half = h.shape[1] // 2
    lo = lax.bitcast_convert_type(h[:, :half].astype(BF16).astype(F32), U32)
    hi = lax.bitcast_convert_type(h[:, half:].astype(BF16).astype(F32), U32)
    hp_ref[...] = (lo >> 16) | (hi & jnp.uint32(0xFFFF0000))

    logits = _dot_split(h, rw_ref[...])
    lt = logits.T[0:n_exp, :]
    score = _sigmoid(lt)
    sel = score + rb_ref[...]
    srow = [sel[e:e + 1, :] for e in range(n_exp)]
    prow = [score[e:e + 1, :] for e in range(n_exp)]

    def group_score(gi):
        v = srow[gi * GROUP_SIZE:(gi + 1) * GROUP_SIZE]
        best = None
        for a in range(GROUP_SIZE):
            for b in range(a + 1, GROUP_SIZE):
                s = v[a] + v[b]
                best = s if best is None else jnp.maximum(best, s)
        return best

    best_g = group_score(0)
    bi = jnp.zeros(best_g.shape, I32)
    for gi in range(1, N_EXPERT_GROUPS):
        gs = group_score(gi)
        upd = gs > best_g
        best_g = jnp.where(upd, gs, best_g)
        bi = jnp.where(upd, gi, bi)

    def pick(rows, j):
        out = rows[j]
        for gi in range(1, N_EXPERT_GROUPS):
            out = jnp.where(bi == gi, rows[gi * GROUP_SIZE + j], out)
        return out

    v = [pick(srow, j) for j in range(GROUP_SIZE)]
    p = [pick(prow, j) for j in range(GROUP_SIZE)]

    def argmax_first(vals):
        best, arg = vals[0], jnp.zeros(vals[0].shape, I32)
        for j in range(1, GROUP_SIZE):
            upd = vals[j] > best
            best = jnp.where(upd, vals[j], best)
            arg = jnp.where(upd, j, arg)
        return arg

    i1 = argmax_first(v)
    i2 = argmax_first([jnp.where(i1 == j, -jnp.inf, v[j]) for j in range(GROUP_SIZE)])
    p1 = sum(jnp.where(i1 == j, p[j], 0.0) for j in range(GROUP_SIZE))
    p2 = sum(jnp.where(i2 == j, p[j], 0.0) for j in range(GROUP_SIZE))
    den = p1 + p2
    e1 = bi * GROUP_SIZE + i1
    e2 = bi * GROUP_SIZE + i2
    idx_ref[0:1, :] = e1
    idx_ref[1:2, :] = e2
    wt_ref[0:1, :] = p1 / den
    wt_ref[1:2, :] = p2 / den

    erow = lax.broadcasted_iota(I32, (n_exp, tm), 0)
    oh1 = (erow == e1).astype(F32)
    oh2 = (erow == e2).astype(F32)
    cnt = oh1 + oh2
    before = (lax.broadcasted_iota(I32, (tm, tm), 0)
              < lax.broadcasted_iota(I32, (tm, tm), 1)).astype(BF16)
    prior = jnp.dot(cnt.astype(BF16), before, preferred_element_type=F32) + carry[...]
    rank_ref[0:1, :] = jnp.sum(oh1 * prior, axis=0, keepdims=True).astype(I32)
    rank_ref[1:2, :] = jnp.sum(oh2 * prior, axis=0, keepdims=True).astype(I32)
    carry[...] = carry[...] + jnp.sum(cnt, axis=1, keepdims=True)
    cnt_ref[...] = jnp.broadcast_to(carry[...], cnt_ref.shape)


def _router(x, g, mod, l, rw_pad, rb_col, geo):
    t, d = x.shape
    r = mod.shape[1]
    n_exp = rb_col.shape[0]
    tm = 256
    return pl.pallas_call(
        functools.partial(_router_kernel, tm=tm, geo=geo, n_exp=n_exp),
        grid=(t // tm,),
        in_specs=[
            pl.BlockSpec((tm, d), lambda i: (i, 0)),
            pl.BlockSpec((1, d), lambda i: (0, 0)),
            pl.BlockSpec((None, r, d), lambda i: (l, 0, 3)),
            pl.BlockSpec((None, r, d), lambda i: (l, 0, 4)),
            pl.BlockSpec(rw_pad.shape, lambda i: (0, 0)),
            pl.BlockSpec(rb_col.shape, lambda i: (0, 0)),
        ],
        out_specs=[
            pl.BlockSpec((tm, d // 2), lambda i: (i, 0)),
            pl.BlockSpec((2, tm), lambda i: (0, i)),
            pl.BlockSpec((2, tm), lambda i: (0, i)),
            pl.BlockSpec((2, tm), lambda i: (0, i)),
            pl.BlockSpec((n_exp, 128), lambda i: (0, 0)),
        ],
        out_shape=[
            jax.ShapeDtypeStruct((t, d // 2), U32),
            jax.ShapeDtypeStruct((2, t), I32),
            jax.ShapeDtypeStruct((2, t), F32),
            jax.ShapeDtypeStruct((2, t), I32),
            jax.ShapeDtypeStruct((n_exp, 128), F32),
        ],
        scratch_shapes=[pltpu.VMEM((n_exp, 1), F32)],
        compiler_params=_cparams(1, 32),
    )(x, g, mod, mod, rw_pad, rb_col)


def _row_copy(src, dst, sem):
    return pltpu.make_async_copy(src, dst, sem)


ZERO_ROWS = 256


def _dispatch_kernel(idx_s, rank_s, off_s, pad_s, h_ref, xs_out, sem, zbuf, zsem, *,
                     tm, t, n_exp, tile, n_tiles):
    base = pl.program_id(0) * tm

    @pl.when(pl.program_id(0) == 0)
    def _():
        zbuf[...] = jnp.zeros(zbuf.shape, U32)

        def for_each_clear(fn):
            for e in range(n_exp):
                start, n = pad_s[e], pad_s[n_exp + e]
                head = (-start) & 7
                for q in range(7):
                    cp = _row_copy(zbuf.at[pl.ds(0, 1)], xs_out.at[pl.ds(start + q, 1)], zsem)
                    pl.when(q < head)(functools.partial(fn, cp))
                pos = pl.multiple_of(start + head, 8)
                rest = n - head
                p = ZERO_ROWS
                while p >= 8:
                    cp = _row_copy(zbuf.at[pl.ds(0, p)], xs_out.at[pl.ds(pos, p)], zsem)
                    pl.when((rest & p) != 0)(functools.partial(fn, cp))
                    pos = pl.multiple_of(pos + (rest & p), 8)
                    p //= 2

            def unused(tl, carry):
                for part in range(tile // ZERO_ROWS):
                    row = pl.multiple_of(tl * tile + part * ZERO_ROWS, ZERO_ROWS)
                    fn(_row_copy(zbuf, xs_out.at[pl.ds(row, ZERO_ROWS)], zsem))
                return carry

            lax.fori_loop(pad_s[2 * n_exp], n_tiles, unused, 0)

        for_each_clear(lambda cp: cp.start())
        for_each_clear(lambda cp: cp.wait())

    def issue(r, carry):
        for k in range(2):
            pos = k * t + base + r
            dst = off_s[idx_s[pos]] + rank_s[pos]
            _row_copy(h_ref.at[pl.ds(r, 1)], xs_out.at[pl.ds(dst, 1)], sem).start()
        return carry

    lax.fori_loop(0, tm, issue, 0, unroll=4)

    def drain(r, carry):
        for k in range(2):
            _row_copy(h_ref.at[pl.ds(0, 1)], xs_out.at[pl.ds(0, 1)], sem).wait()
        return carry

    lax.fori_loop(0, tm, drain, 0, unroll=4)


def _dispatch(hp, idx_flat, rank_flat, off, pad_meta, n_rows, n_exp, tile):
    t, w = hp.shape
    tm = 256
    return pl.pallas_call(
        functools.partial(_dispatch_kernel, tm=tm, t=t, n_exp=n_exp, tile=tile,
                          n_tiles=n_rows // tile),
        grid_spec=pltpu.PrefetchScalarGridSpec(
            num_scalar_prefetch=4,
            grid=(t // tm,),
            in_specs=[pl.BlockSpec((tm, w), lambda i, *_: (i, 0))],
            out_specs=pl.BlockSpec(memory_space=pl.ANY),
            scratch_shapes=[pltpu.SemaphoreType.DMA(()), pltpu.VMEM((ZERO_ROWS, w), U32),
                            pltpu.SemaphoreType.DMA(())],
        ),
        out_shape=jax.ShapeDtypeStruct((n_rows, w), U32),
        compiler_params=_cparams(1, 32),
    )(idx_flat, rank_flat, off, pad_meta, hp)


def _expert_changed(te_s, i):
    return (i == 0) | (te_s[i] != te_s[jnp.maximum(i - 1, 0)])


def _up_kernel(te_s, nu_s, x_ref, wg_ref, wu_ref, o_ref, wgb, wub):
    i = pl.program_id(1)

    @pl.when(_expert_changed(te_s, i))
    def _():
        wgb[...] = wg_ref[...].astype(BF16)
        wub[...] = wu_ref[...].astype(BF16)

    @pl.when(i < nu_s[0])
    def _():
        pk = x_ref[...]
        lo = lax.bitcast_convert_type(pk << 16, F32)
        hi = lax.bitcast_convert_type(pk & jnp.uint32(0xFFFF0000), F32)
        x = jnp.concatenate([lo, hi], axis=1).astype(BF16)
        gt = jnp.dot(x, wgb[...], preferred_element_type=F32)
        up = jnp.dot(x, wub[...], preferred_element_type=F32)
        o_ref[...] = (_silu(gt) * up).astype(BF16)

    @pl.when(i >= nu_s[0])
    def _():
        o_ref[...] = jnp.zeros(o_ref.shape, BF16)


def _moe_up(xs, te, nu, w_gate, w_up, l, tm):
    n_rows, w = xs.shape
    d, f = w_gate.shape[2], w_gate.shape[3]
    tf = 512
    wspec = pl.BlockSpec((None, None, d, tf), lambda j, i, te_s, nu_s: (l, te_s[i], 0, j))
    return pl.pallas_call(
        _up_kernel,
        grid_spec=pltpu.PrefetchScalarGridSpec(
            num_scalar_prefetch=2,
            grid=(f // tf, n_rows // tm),
            in_specs=[pl.BlockSpec((tm, w), lambda j, i, *_: (i, 0)), wspec, wspec],
            out_specs=pl.BlockSpec((tm, tf), lambda j, i, *_: (i, j)),
            scratch_shapes=[pltpu.VMEM((d, tf), BF16)] * 2,
        ),
        out_shape=jax.ShapeDtypeStruct((n_rows, f), BF16),
        compiler_params=_cparams(2, VMEM_BUDGET_MIB),
    )(te, nu, xs, w_gate, w_up)


def _down_kernel(te_s, nu_s, a_ref, w_ref, o_ref, wb):
    i = pl.program_id(1)

    @pl.when(_expert_changed(te_s, i))
    def _():
        wb[...] = w_ref[...].astype(BF16)

    @pl.when(i < nu_s[0])
    def _():
        o_ref[...] = jnp.dot(a_ref[...], wb[...], preferred_element_type=F32)

    @pl.when(i >= nu_s[0])
    def _():
        o_ref[...] = jnp.zeros(o_ref.shape, F32)


def _moe_down(act, te, nu, w_down, l, tm):
    n_rows, f = act.shape
    d = w_down.shape[3]
    tn = min(d, 2048)
    return pl.pallas_call(
        _down_kernel,
        grid_spec=pltpu.PrefetchScalarGridSpec(
            num_scalar_prefetch=2,
            grid=(d // tn, n_rows // tm),
            in_specs=[pl.BlockSpec((tm, f), lambda j, i, *_: (i, 0)),
                      pl.BlockSpec((None, None, f, tn), lambda j, i, te_s, nu_s: (l, te_s[i], 0, j))],
            out_specs=pl.BlockSpec((tm, tn), lambda j, i, *_: (i, j)),
            scratch_shapes=[pltpu.VMEM((f, tn), BF16)],
        ),
        out_shape=jax.ShapeDtypeStruct((n_rows, d), F32),
        compiler_params=_cparams(2, 48),
    )(te, nu, act, w_down)


def _combine_kernel(idx_s, rank_s, off_s, x_ref, wt_ref, g2_ref, ng_ref, sh_ref, s_ref, ys_ref,
                    *rest, tm, t, geo, final):
    o_ref, o2_ref, buf, sem = rest
    i = pl.program_id(0)
    n_steps = t // tm
    slot = lax.rem(i, 2)

    def gather(tile, dst_slot):
        def issue(r, carry):
            for k in range(2):
                pos = k * t + tile * tm + r
                src = off_s[idx_s[pos]] + rank_s[pos]
                _row_copy(ys_ref.at[pl.ds(src, 1)], buf.at[dst_slot, k, pl.ds(r, 1)],
                          sem.at[dst_slot]).start()
            return carry

        lax.fori_loop(0, tm, issue, 0, unroll=4)

    @pl.when(i == 0)
    def _():
        gather(0, 0)

    @pl.when(i + 1 < n_steps)
    def _():
        gather(i + 1, 1 - slot)

    def drain(r, carry):
        for k in range(2):
            _row_copy(ys_ref.at[pl.ds(0, 1)], buf.at[slot, k, pl.ds(0, 1)], sem.at[slot]).wait()
        return carry

    lax.fori_loop(0, tm, drain, 0, unroll=4)

    row = _mod_row(i * tm, *geo)
    wt = wt_ref[...]
    moe = wt[:, 0:1] * buf[slot, 0] + wt[:, 1:2] * buf[slot, 1]
    xn = x_ref[...] + g2_ref[pl.ds(row, 1), :] * moe
    y = xn * lax.rsqrt(jnp.mean(xn * xn, axis=-1, keepdims=True) + EPS) * ng_ref[...]
    if final:
        @pl.when(i * tm < geo[0])
        def _():
            o_ref[...] = y

        @pl.when(i * tm >= geo[0])
        def _():
            o2_ref[...] = y
    else:
        o_ref[...] = xn
        o2_ref[...] = (y * (1.0 + s_ref[pl.ds(row, 1), :]) + sh_ref[pl.ds(row, 1), :]).astype(BF16)


def _combine(x, wt_col, mod, l, norm_g, ys, idx_flat, rank_flat, off, geo, final):
    t, d = x.shape
    r = mod.shape[1]
    tm = 256
    l_next = l if final else l + 1
    if final:
        n_ctx_tiles = geo[0] // tm
        out_specs = [pl.BlockSpec((tm, d), lambda i, *_: (jnp.minimum(i, n_ctx_tiles - 1), 0)),
                     pl.BlockSpec((tm, d), lambda i, *_: (jnp.maximum(i - n_ctx_tiles, 0), 0))]
        out_shape = [jax.ShapeDtypeStruct((geo[0], d), F32),
                     jax.ShapeDtypeStruct((t - geo[0], d), F32)]
    else:
        out_specs = [pl.BlockSpec((tm, d), lambda i, *_: (i, 0))] * 2
        out_shape = [jax.ShapeDtypeStruct((t, d), F32), jax.ShapeDtypeStruct((t, d), BF16)]
    return pl.pallas_call(
        functools.partial(_combine_kernel, tm=tm, t=t, geo=geo, final=final),
        grid_spec=pltpu.PrefetchScalarGridSpec(
            num_scalar_prefetch=3,
            grid=(t // tm,),
            in_specs=[
                pl.BlockSpec((tm, d), lambda i, *_: (i, 0)),
                pl.BlockSpec((tm, 2), lambda i, *_: (i, 0)),
                pl.BlockSpec((None, r, d), lambda i, *_: (l, 0, 5)),
                pl.BlockSpec((1, d), lambda i, *_: (0, 0)),
                pl.BlockSpec((None, r, d), lambda i, *_: (l_next, 0, 0)),
                pl.BlockSpec((None, r, d), lambda i, *_: (l_next, 0, 1)),
                pl.BlockSpec(memory_space=pl.ANY),
            ],
            out_specs=out_specs,
            scratch_shapes=[pltpu.VMEM((2, 2, tm, d), F32), pltpu.SemaphoreType.DMA((2,))],
        ),
        out_shape=out_shape,
        compiler_params=_cparams(1, 48),
    )(idx_flat, rank_flat, off, x, wt_col, mod, norm_g, mod, mod, ys)


def kernel(x_prompt, x_sample, state_ssd, c, c_ctx, ada_w, ada_b, norm1_g, norm2_g, w_in,
           ssd_conv_w, ssd_conv_b, ssd_dt_bias, ssd_a_log, ssd_d, ssd_norm_g, ssd_out,
           sc_conv_w, sc_conv_b, sc_out, cf_conv_w, cf_conv_b, cf_ln_g, cf_ln_b, cf_out, w_o,
           router_w, router_bias, moe_w_gate, moe_w_up, moe_w_down, final_g):
    n_ctx, seq, d = x_prompt.shape
    n_dec, dec_seq, _ = x_sample.shape
    depth = w_in.shape[0]
    n_heads = d // HEAD_DIM
    gn = SSD_GROUPS * SSD_STATE
    xbc_w = d + 2 * gn
    sc_w = sc_out.shape[1]
    cf_w = cf_out.shape[1]
    n_exp = router_w.shape[1]
    t_ctx = n_ctx * seq
    t = t_ctx + n_dec * dec_seq
    geo = (t_ctx, dec_seq, n_dec)
    off_xbc = d
    off_dt = off_xbc + xbc_w
    off_sc = off_dt + 2 * n_heads
    off_cf = off_sc + 3 * sc_w
    off_gate = off_cf + 2 * cf_w

    x_parts = (x_prompt.reshape(t_ctx, d), x_sample.reshape(n_dec * dec_seq, d))
    n_mod_rows = -(-(n_dec + 1) // 8) * 8
    cond = jnp.concatenate([c, c_ctx[None, :],
                            jnp.zeros((n_mod_rows - n_dec - 1, d), F32)], axis=0)
    mod = _ada_table(cond, ada_w, ada_b)

    h0_lat = state_ssd.reshape(n_dec, depth, 2, n_heads // 2, PAIR, SSD_STATE)
    state_buf = jnp.zeros((n_ctx, depth, 2, n_heads // 2, PAIR, SSD_STATE), F32)
    rw_pad = jnp.pad(router_w, ((0, 0), (0, 128 - n_exp)))
    rb_col = router_bias.reshape(n_exp, 1)
    tm_e = 512
    n_rows = 2 * t + n_exp * tm_e
    n_tiles = n_rows // tm_e

    h = _normmod(x_parts, norm1_g[0:1], mod, 0, geo)
    y_final = None
    for l in range(depth):
        z = _proj(h, w_in, l, [0], d, 512, 1024, _epi_identity, [], [], F32)
        xbc = _proj(
            h, w_in, l, [off_xbc], xbc_w, 512, dec_seq,
            functools.partial(_epi_xbc, tm=dec_seq, t_ctx=t_ctx, seq=seq, dec_seq=dec_seq),
            [ssd_conv_w, ssd_conv_b.reshape(depth, 1, xbc_w)],
            [pl.BlockSpec((None, 3, 512), lambda j, i: (l, 0, j)),
             pl.BlockSpec((None, 1, 512), lambda j, i: (l, 0, j))], F32)
        dt = _proj(h, w_in, l, [off_dt], 2 * n_heads, 2 * n_heads, 1024, _epi_dt,
                   [ssd_dt_bias.reshape(depth, 1, 2 * n_heads)],
                   [pl.BlockSpec((None, 1, 2 * n_heads), lambda j, i: (l, 0, 0))], F32)
        v = _proj(
            h, w_in, l, [off_sc, off_sc + sc_w, off_sc + 2 * sc_w], sc_w, 256, 1024,
            functools.partial(_epi_sc, tm=1024, t_ctx=t_ctx, seq=seq),
            [sc_conv_w, sc_conv_b.reshape(depth, 1, sc_w)],
            [pl.BlockSpec((None, 3, 256), lambda j, i: (l, 0, j)),
             pl.BlockSpec((None, 1, 256), lambda j, i: (l, 0, j))], BF16)
        u_cf = _proj(h, w_in, l, [off_cf, off_cf + cf_w], cf_w, 256, 1024, _epi_cf, [], [], F32)
        gates = _proj(h, w_in, l, [off_gate], 3 * d, 512, 1024, _epi_sigmoid, [], [], BF16)

        alog = ssd_a_log[l].reshape(1, 2 * n_heads)
        dskip = jnp.repeat(ssd_d[l], HEAD_DIM).reshape(1, d)
        ng = ssd_norm_g[l:l + 1]
        scan = dict(d=d, l=l, shape=(n_ctx, seq, n_dec, dec_seq))
        yf, state_buf = _ssd_pass(xbc, dt, alog, h0_lat, state_buf, reverse=False, **scan)
        y_ssd, state_buf = _ssd_pass(xbc, dt, alog, h0_lat, state_buf, reverse=True,
                                     fused_in=(yf, z, dskip, ng), **scan)

        u = _cfconv(u_cf, cf_conv_w[l], cf_conv_b[l:l + 1], cf_ln_g[l:l + 1], cf_ln_b[l:l + 1],
                    t_ctx=t_ctx, seq=seq, dec_seq=dec_seq)

        merged = _merge(y_ssd, v, u, gates, ssd_out, sc_out, cf_out, l)
        x = _wo(merged, w_o, x_parts, mod, l, geo)

        hp, idx, wt, rank, cnt = _router(x, norm2_g[l:l + 1], mod, l, rw_pad, rb_col, geo)
        counts = cnt[:, 0].astype(I32)
        tiles = (counts + tm_e - 1) // tm_e
        tile_end = jnp.cumsum(tiles)
        off = (tile_end - tiles) * tm_e
        n_used = tile_end[-1]
        tile_id = jnp.minimum(jnp.arange(n_tiles, dtype=I32), n_used - 1)
        te = jnp.sum(tile_id[:, None] >= tile_end[None, :], axis=1).astype(I32)
        nu = n_used.reshape(1).astype(I32)
        idx_flat = idx.reshape(2 * t)
        rank_flat = rank.reshape(2 * t)
        pad_meta = jnp.concatenate([off + counts, tiles * tm_e - counts, nu]).astype(I32)
        xs = _dispatch(hp, idx_flat, rank_flat, off, pad_meta, n_rows, n_exp, tm_e)
        act = _moe_up(xs, te, nu, moe_w_gate, moe_w_up, l, tm_e)
        ys = _moe_down(act, te, nu, moe_w_down, l, tm_e)
        final = l == depth - 1
        ng_next = final_g.reshape(1, d) if final else norm1_g[l + 1:l + 2]
        outs = _combine(x, wt.T, mod, l, ng_next, ys, idx_flat, rank_flat, off, geo, final)
        if final:
            y_final = outs
        else:
            x, h = outs
            x_parts = (x, x)

    y_prompt = y_final[0].reshape(n_ctx, seq, d)
    y_sample = y_final[1].reshape(n_dec, dec_seq, d)
    new_state = state_buf.reshape(n_ctx, depth, 2, n_heads, HEAD_DIM, SSD_STATE)
    return (y_prompt, y_sample, new_state)
```

```python
import functools

import jax
import jax.numpy as jnp
import numpy as np
from jax import lax
from jax.experimental import pallas as pl
from jax.experimental.pallas import tpu as pltpu

F32 = jnp.float32
BF16 = jnp.bfloat16
I32 = jnp.int32
U32 = jnp.uint32

EPS = 1e-6
GRID_W = 64
HEAD_DIM = 64
PAIR = 2 * HEAD_DIM
SSD_GROUPS = 8
SSD_STATE = 128
CHUNK = 128
N_EXPERT_GROUPS = 4
GROUP_SIZE = 4
MIB = 1024 * 1024
VMEM_BUDGET_MIB = 58


def _cparams(n_axes, vmem_mib):
    return pltpu.CompilerParams(
        dimension_semantics=("arbitrary",) * n_axes,
        vmem_limit_bytes=min(vmem_mib, VMEM_BUDGET_MIB) * MIB)


def _sigmoid(x):
    return 1.0 / (1.0 + jnp.exp(-x))


def _silu(x):
    return x * _sigmoid(x)


def _mod_row(start, t_ctx, dec_seq, n_dec):
    return jnp.where(start < t_ctx, n_dec, lax.div(start - t_ctx, dec_seq))


def _dot_split(a, b):
    a_hi, b_hi = a.astype(BF16), b.astype(BF16)
    a_lo = (a - a_hi.astype(F32)).astype(BF16)
    b_lo = (b - b_hi.astype(F32)).astype(BF16)
    return (jnp.dot(a_hi, b_hi, preferred_element_type=F32)
            + jnp.dot(a_lo, b_hi, preferred_element_type=F32)
            + jnp.dot(a_hi, b_lo, preferred_element_type=F32))


def _ada_kernel(c_ref, w_ref, b_ref, o_ref):
    o_ref[...] = _dot_split(_silu(c_ref[...]), w_ref[...]) + b_ref[...]


def _ada_table(cond, ada_w, ada_b):
    depth, d, n = ada_w.shape
    r = cond.shape[0]
    tn = 1024
    return pl.pallas_call(
        _ada_kernel,
        grid=(depth, n // tn),
        in_specs=[
            pl.BlockSpec((r, d), lambda l, j: (0, 0)),
            pl.BlockSpec((None, d, tn), lambda l, j: (l, 0, j)),
            pl.BlockSpec((None, 1, tn), lambda l, j: (l, 0, j)),
        ],
        out_specs=pl.BlockSpec((None, r, tn), lambda l, j: (l, 0, j)),
        out_shape=jax.ShapeDtypeStruct((depth, r, n), F32),
        compiler_params=_cparams(2, 56),
    )(cond, ada_w, ada_b.reshape(depth, 1, n))


def _split_specs(x_parts, tm, width, n_ctx_tiles, tile_axis, col_axis=None):
    xa, xb = x_parts
    b_off = n_ctx_tiles if xa is xb else 0

    def col(ids):
        return 0 if col_axis is None else ids[col_axis]

    def amap(*ids):
        return (jnp.minimum(ids[tile_axis], n_ctx_tiles - 1), col(ids))

    def bmap(*ids):
        return (b_off + jnp.maximum(ids[tile_axis] - n_ctx_tiles, 0), col(ids))

    return [pl.BlockSpec((tm, width), amap), pl.BlockSpec((tm, width), bmap)]


def _normmod_kernel(xa_ref, xb_ref, g_ref, sh_ref, s_ref, o_ref, *, tm, geo):
    start = pl.program_id(0) * tm
    row = _mod_row(start, *geo)

    def emit(x_ref):
        x = x_ref[...]
        y = x * lax.rsqrt(jnp.mean(x * x, axis=-1, keepdims=True) + EPS) * g_ref[...]
        o_ref[...] = (y * (1.0 + s_ref[pl.ds(row, 1), :]) + sh_ref[pl.ds(row, 1), :]).astype(BF16)

    pl.when(start < geo[0])(lambda: emit(xa_ref))
    pl.when(start >= geo[0])(lambda: emit(xb_ref))


def _normmod(x_parts, g, mod, l, geo):
    d = x_parts[0].shape[1]
    t = geo[0] + geo[1] * geo[2]
    r = mod.shape[1]
    tm = 256
    return pl.pallas_call(
        functools.partial(_normmod_kernel, tm=tm, geo=geo),
        grid=(t // tm,),
        in_specs=_split_specs(x_parts, tm, d, geo[0] // tm, 0) + [
            pl.BlockSpec((1, d), lambda i: (0, 0)),
            pl.BlockSpec((None, r, d), lambda i: (l, 0, 0)),
            pl.BlockSpec((None, r, d), lambda i: (l, 0, 1)),
        ],
        out_specs=pl.BlockSpec((tm, d), lambda i: (i, 0)),
        out_shape=jax.ShapeDtypeStruct((t, d), BF16),
        compiler_params=_cparams(1, 32),
    )(*x_parts, g, mod, mod)


def _proj_kernel(*refs, n_dots, n_extra, epilogue):
    h_ref = refs[0]
    w_refs = refs[1:1 + n_dots]
    extra = refs[1 + n_dots:1 + n_dots + n_extra]
    o_ref = refs[1 + n_dots + n_extra]
    wb = refs[2 + n_dots + n_extra:]
    i = pl.program_id(1)

    @pl.when(i == 0)
    def _():
        for d in range(n_dots):
            wb[d][...] = w_refs[d][...].astype(BF16)

    h = h_ref[...]
    accs = [jnp.dot(h, wb[d][...], preferred_element_type=F32) for d in range(n_dots)]
    o_ref[...] = epilogue(accs, extra, i).astype(o_ref.dtype)


def _proj(h, w_in, l, col_offs, n_cols, tn, tm, epilogue, extra, extra_specs, out_dtype):
    t, k = h.shape
    n_dots = len(col_offs)

    def wspec(off):
        return pl.BlockSpec((pl.Squeezed(), pl.Element(k), pl.Element(tn)),
                            lambda j, i: (l, 0, pl.multiple_of(off + j * tn, 128)))

    out_bytes = jnp.dtype(out_dtype).itemsize
    vmem = (2 * tm * k * 2 + n_dots * k * tn * (2 * 4 + 2) + 2 * tm * tn * out_bytes
            + 3 * n_dots * tm * tn * 4) // MIB + 4
    return pl.pallas_call(
        functools.partial(_proj_kernel, n_dots=n_dots, n_extra=len(extra), epilogue=epilogue),
        grid=(n_cols // tn, t // tm),
        in_specs=[pl.BlockSpec((tm, k), lambda j, i: (i, 0))]
        + [wspec(off) for off in col_offs] + list(extra_specs),
        out_specs=pl.BlockSpec((tm, tn), lambda j, i: (i, j)),
        out_shape=jax.ShapeDtypeStruct((t, n_cols), out_dtype),
        scratch_shapes=[pltpu.VMEM((k, tn), BF16)] * n_dots,
        compiler_params=_cparams(2, vmem),
    )(h, *([w_in] * n_dots), *extra)


def _conv3_rows(u, w_ref, b_ref, period):
    rows = u.shape[0]
    pos = lax.broadcasted_iota(I32, u.shape, 0) & (period - 1)
    prev = jnp.where(pos == 0, 0.0, pltpu.roll(u, 1, 0))
    nxt = jnp.where(pos == period - 1, 0.0, pltpu.roll(u, rows - 1, 0))
    return w_ref[0:1, :] * prev + w_ref[1:2, :] * u + w_ref[2:3, :] * nxt + b_ref[...]


def _epi_identity(accs, extra, i):
    return accs[0]


def _epi_sigmoid(accs, extra, i):
    return _sigmoid(accs[0])


def _epi_xbc(accs, extra, i, *, tm, t_ctx, seq, dec_seq):
    period = jnp.where(i * tm < t_ctx, seq, dec_seq)
    return _silu(_conv3_rows(accs[0], extra[0], extra[1], period))


def _epi_dt(accs, extra, i):
    x = accs[0] + extra[0][...]
    return jnp.maximum(x, 0.0) + jnp.log1p(jnp.exp(-jnp.abs(x)))


def _epi_sc(accs, extra, i, *, tm, t_ctx, seq):
    bg, cg, hv = accs
    period = jnp.where(i * tm < t_ctx, seq, GRID_W)
    return bg * _conv3_rows(cg * hv, extra[0], extra[1], period)


def _epi_cf(accs, extra, i):
    return accs[0] * _sigmoid(accs[1])


TBL_BLOCK, TBL_FIRST, TBL_LAST, TBL_LATENT, TBL_H0, TBL_STATE, TBL_FIELDS = 0, 1, 2, 3, 4, 5, 6


def _ssd_kernel(tbl, *refs, reverse, fused, n_heads, n_steps):
    it = iter(refs)
    xs_ref, b_ref, c_ref, dt_ref, alog_ref, expand_ref, h0_ref = (next(it) for _ in range(7))
    if fused:
        yf_ref, z_ref, dskip_ref, ng_ref = (next(it) for _ in range(4))
    next(it)
    y_ref, st_ref, h_scr = next(it), next(it), next(it)
    ybuf = next(it) if fused else None

    step = pl.program_id(0)
    first = tbl[TBL_FIRST * n_steps + step] == 1
    latent = tbl[TBL_LATENT * n_steps + step] == 1

    @pl.when(first & latent)
    def _():
        h_scr[...] = h0_ref[...]

    @pl.when(first & jnp.logical_not(latent))
    def _():
        h_scr[...] = jnp.zeros(h_scr.shape, F32)

    q = CHUNK
    dt = dt_ref[...]
    dta = dt * (-jnp.exp(alog_ref[...]))
    r_i = lax.broadcasted_iota(I32, (q, q), 0)
    c_i = lax.broadcasted_iota(I32, (q, q), 1)
    tri = (c_i >= r_i) if reverse else (c_i <= r_i)
    d_hi = dta.astype(BF16)
    rem = dta - d_hi.astype(F32)
    d_mid = rem.astype(BF16)
    d_lo = (rem - d_mid.astype(F32)).astype(BF16)
    parts = jnp.dot(tri.astype(BF16), jnp.concatenate([d_hi, d_mid, d_lo], axis=1),
                    preferred_element_type=F32)
    nh2 = dta.shape[1]
    acum = parts[:, 0:nh2] + parts[:, nh2:2 * nh2] + parts[:, 2 * nh2:3 * nh2]
    acum_t = acum.T
    last = acum[0:1, :] if reverse else acum[q - 1:q, :]
    expa = jnp.exp(acum)
    dend = jnp.exp(last - acum)
    cdec = jnp.exp(last)
    lane_lo = lax.broadcasted_iota(I32, (q, PAIR), 1) < HEAD_DIM
    row_lo = lax.broadcasted_iota(I32, (PAIR, SSD_STATE), 0) < HEAD_DIM
    d0 = n_heads if reverse else 0
    hg = n_heads // SSD_GROUPS
    nt = (((1,), (1,)), ((), ()))

    per_head = jnp.concatenate([dt, dt * dend, expa], axis=0)
    hi = per_head.astype(BF16)
    lo = (per_head - hi.astype(F32)).astype(BF16)
    on_lanes = jnp.dot(jnp.concatenate([hi, lo], axis=1), expand_ref[...],
                       preferred_element_type=F32)
    xs_all = xs_ref[...]
    xdt_all = xs_all * on_lanes[0:q]
    xw_all = xs_all * on_lanes[q:2 * q]
    expa_all = on_lanes[2 * q:3 * q]

    pairs_g = hg // 2
    for g in range(SSD_GROUPS):
        cg = c_ref[:, g * SSD_STATE:(g + 1) * SSD_STATE].astype(BF16)
        bg = b_ref[:, g * SSD_STATE:(g + 1) * SSD_STATE].astype(BF16)
        cb = lax.dot_general(cg, bg, nt, preferred_element_type=F32)
        p0 = g * pairs_g
        gslab = slice(p0 * PAIR, (p0 + pairs_g) * PAIR)
        hgrp = jnp.concatenate([h_scr[p0 + jp] for jp in range(pairs_g)], axis=0)
        y_in = lax.dot_general(cg, hgrp.astype(BF16), nt,
                               preferred_element_type=F32) * expa_all[:, gslab]
        st = jnp.dot(xw_all[:, gslab].T.astype(BF16), bg, preferred_element_type=F32)
        for jp in range(pairs_g):
            pair = p0 + jp
            hx = d0 + 2 * pair
            rows = slice(jp * PAIR, (jp + 1) * PAIR)
            keep = jnp.where(row_lo, cdec[:, hx:hx + 1], cdec[:, hx + 1:hx + 2])
            h_scr[pair] = keep * hgrp[rows] + st[rows]

            def decay_w(hh):
                diff = acum[:, hh:hh + 1] - acum_t[hh:hh + 1, :]
                return (cb * jnp.exp(jnp.where(tri, diff, -jnp.inf))).astype(BF16)

            slab = slice(pair * PAIR, (pair + 1) * PAIR)
            w = jnp.concatenate([decay_w(hx), decay_w(hx + 1)], axis=1)
            xdt = xdt_all[:, slab]
            xbd = jnp.concatenate([jnp.where(lane_lo, xdt, 0.0),
                                   jnp.where(lane_lo, 0.0, xdt)], axis=0).astype(BF16)
            y = jnp.dot(w, xbd, preferred_element_type=F32) + y_in[:, rows]
            if fused:
                ybuf[:, slab] = y
            else:
                y_ref[:, slab] = y

    @pl.when((tbl[TBL_LAST * n_steps + step] == 1) & jnp.logical_not(latent))
    def _():
        st_ref[...] = h_scr[...]

    if fused:
        yt = ybuf[...] + yf_ref[...] + dskip_ref[...] * xs_all
        yt = yt * _silu(z_ref[...])
        yn = yt * lax.rsqrt(jnp.mean(yt * yt, axis=-1, keepdims=True) + EPS) * ng_ref[...]
        y_ref[...] = yn.astype(BF16)


def _ssd_table(n_ctx, seq, n_dec, dec_seq, reverse):
    steps = []
    blk = 0
    for n_seq, length, latent in ((n_ctx, seq, 0), (n_dec, dec_seq, 1)):
        n_chunks = length // CHUNK
        for b in range(n_seq):
            for c in range(n_chunks):
                steps.append((blk + c, int(c == 0), int(c == n_chunks - 1), latent,
                              b if latent else 0, n_ctx - 1 if latent else b))
            blk += n_chunks
    if reverse:
        steps = [(s[0], s[2], s[1], s[3], s[4], s[5]) for s in reversed(steps)]
    return np.asarray(steps, np.int32).T.reshape(-1), len(steps)


def _ssd_pass(xbc, dt, alog, h0, state_buf, *, d, l, shape, reverse, fused_in=None):
    n_heads = d // HEAD_DIM
    gn = SSD_GROUPS * SSD_STATE
    direction = 1 if reverse else 0
    fused = fused_in is not None
    t = xbc.shape[0]
    table, n_steps = _ssd_table(*shape, reverse)

    def field(f):
        return lambda s, tbl: tbl[f * n_steps + s]

    blk = field(TBL_BLOCK)
    row_spec = pl.BlockSpec((CHUNK, d), lambda s, tbl: (blk(s, tbl), 0))
    vec_spec = pl.BlockSpec((1, d), lambda s, tbl: (0, 0))
    state_block = (None, None, None, n_heads // 2, PAIR, SSD_STATE)
    in_specs = [
        row_spec,
        pl.BlockSpec((CHUNK, gn), lambda s, tbl: (blk(s, tbl), d // gn)),
        pl.BlockSpec((CHUNK, gn), lambda s, tbl: (blk(s, tbl), d // gn + 1)),
        pl.BlockSpec((CHUNK, 2 * n_heads), lambda s, tbl: (blk(s, tbl), 0)),
        pl.BlockSpec((1, 2 * n_heads), lambda s, tbl: (0, 0)),
        pl.BlockSpec((4 * n_heads, d), lambda s, tbl: (0, 0)),
        pl.BlockSpec(state_block,
                     lambda s, tbl: (field(TBL_H0)(s, tbl), l, direction, 0, 0, 0)),
    ]
    expand = np.zeros((2 * n_heads, d), np.float32)
    expand[direction * n_heads + np.arange(d) // HEAD_DIM, np.arange(d)] = 1.0
    expand = np.concatenate([expand, expand], axis=0)
    args = [jnp.asarray(table), xbc, xbc, xbc, dt, alog, jnp.asarray(expand, BF16), h0]
    if fused:
        in_specs += [row_spec, row_spec, vec_spec, vec_spec]
        args += list(fused_in)
    in_specs.append(pl.BlockSpec(memory_space=pl.ANY))
    args.append(state_buf)
    scratch = [pltpu.VMEM((n_heads // 2, PAIR, SSD_STATE), F32)]
    if fused:
        scratch.append(pltpu.VMEM((CHUNK, d), F32))
    return pl.pallas_call(
        functools.partial(_ssd_kernel, reverse=reverse, fused=fused, n_heads=n_heads,
                          n_steps=n_steps),
        grid_spec=pltpu.PrefetchScalarGridSpec(
            num_scalar_prefetch=1,
            grid=(n_steps,),
            in_specs=in_specs,
            out_specs=[
                row_spec,
                pl.BlockSpec(state_block,
                             lambda s, tbl: (field(TBL_STATE)(s, tbl), l, direction, 0, 0, 0)),
            ],
            scratch_shapes=scratch,
        ),
        out_shape=[jax.ShapeDtypeStruct((t, d), BF16 if fused else F32),
                   jax.ShapeDtypeStruct(state_buf.shape, F32)],
        input_output_aliases={len(args) - 1: 1},
        compiler_params=_cparams(1, 40),
    )(*args)


def _ln_swish_store(conv, g_ref, b_ref, o_ref):
    x = conv[...]
    mu = jnp.mean(x, axis=-1, keepdims=True)
    xc = x - mu
    var = jnp.mean(xc * xc, axis=-1, keepdims=True)
    y = xc * lax.rsqrt(var + EPS) * g_ref[...] + b_ref[...]
    o_ref[...] = _silu(y).astype(o_ref.dtype)


def _cfconv_kernel(u_ref, w_ref, b_ref, g_ref, be_ref, o_ref, pad, shifted, conv, *,
                   seq, block, width, k, n_ctx_blocks):
    hk = k // 2
    lead = 16
    span = seq + 24
    rows = block // GRID_W
    i = pl.program_id(0)

    @pl.when(i < n_ctx_blocks)
    def _():
        pad[0:lead, :] = jnp.zeros((lead, width), F32)
        pad[lead + seq:2 * lead + seq, :] = jnp.zeros((lead, width), F32)
        for sq in range(block // seq):
            pad[lead:lead + seq, :] = u_ref[sq * seq:(sq + 1) * seq, :]

            def body(cbi, carry):
                ds = pl.ds(pl.multiple_of(cbi * 128, 128), 128)
                for res in range(1, 8):
                    shifted[res] = pad[pl.ds(res, span), ds]
                acc = jnp.zeros((seq, 128), F32)
                for tap in range(k):
                    base, res = divmod(lead - hk + tap, 8)
                    if res == 0:
                        src = pad[pl.ds(8 * base, seq), ds]
                    else:
                        src = shifted[res, pl.ds(8 * base, seq), :]
                    acc = acc + w_ref[tap:tap + 1, ds] * src
                conv[pl.ds(sq * seq, seq), ds] = acc + b_ref[:, ds]
                return carry

            lax.fori_loop(0, width // 128, body, 0)

    @pl.when(i >= n_ctx_blocks)
    def _():
        def body(cbi, carry):
            ds = pl.ds(pl.multiple_of(cbi * 128, 128), 128)
            for r in range(rows):
                acc = None
                for tap in range(k):
                    src = r + tap - hk
                    if 0 <= src < rows:
                        term = w_ref[tap:tap + 1, ds] * u_ref[pl.ds(src * GRID_W, GRID_W), ds]
                        acc = term if acc is None else acc + term
                conv[pl.ds(r * GRID_W, GRID_W), ds] = acc + b_ref[:, ds]
            return carry

        lax.fori_loop(0, width // 128, body, 0)

    _ln_swish_store(conv, g_ref, be_ref, o_ref)


def _cfconv(u, w, b, g, be, *, t_ctx, seq, dec_seq):
    t, width = u.shape
    k = w.shape[0]
    assert dec_seq % seq == 0 and t_ctx % dec_seq == 0 and k // 2 < 16
    vec = pl.BlockSpec((1, width), lambda i: (0, 0))
    return pl.pallas_call(
        functools.partial(_cfconv_kernel, seq=seq, block=dec_seq, width=width, k=k,
                          n_ctx_blocks=t_ctx // dec_seq),
        grid=(t // dec_seq,),
        in_specs=[pl.BlockSpec((dec_seq, width), lambda i: (i, 0)),
                  pl.BlockSpec((k, width), lambda i: (0, 0)), vec, vec, vec],
        out_specs=pl.BlockSpec((dec_seq, width), lambda i: (i, 0)),
        out_shape=jax.ShapeDtypeStruct((t, width), BF16),
        scratch_shapes=[pltpu.VMEM((seq + 32, width), F32),
                        pltpu.VMEM((8, seq + 24, 128), F32),
                        pltpu.VMEM((dec_seq, width), F32)],
        compiler_params=_cparams(1, 48),
    )(u, w, b, g, be)


def _merge_kernel(y_ref, v_ref, u_ref, g0_ref, g1_ref, g2_ref, w0_ref, w1_ref, w2_ref,
                  o_ref, wb0, wb1, wb2):
    @pl.when(pl.program_id(1) == 0)
    def _():
        wb0[...] = w0_ref[...].astype(BF16)
        wb1[...] = w1_ref[...].astype(BF16)
        wb2[...] = w2_ref[...].astype(BF16)

    o0 = jnp.dot(y_ref[...], wb0[...], preferred_element_type=F32)
    o1 = jnp.dot(v_ref[...], wb1[...], preferred_element_type=F32)
    o2 = jnp.dot(u_ref[...], wb2[...], preferred_element_type=F32)
    merged = (g0_ref[...].astype(F32) * o0 + g1_ref[...].astype(F32) * o1
              + g2_ref[...].astype(F32) * o2)
    o_ref[...] = merged.astype(BF16)


def _merge(y, v, u, gates, ssd_out, sc_out, cf_out, l):
    t, d = y.shape
    kv, ku = v.shape[1], u.shape[1]
    tm, tn = 512, 512
    nj = d // tn
    one = pl.Buffered(1)

    def wspec(k):
        return pl.BlockSpec((None, k, tn), lambda j, i: (l, 0, j), pipeline_mode=one)

    return pl.pallas_call(
        _merge_kernel,
        grid=(nj, t // tm),
        in_specs=[
            pl.BlockSpec((tm, d), lambda j, i: (i, 0)),
            pl.BlockSpec((tm, kv), lambda j, i: (i, 0)),
            pl.BlockSpec((tm, ku), lambda j, i: (i, 0)),
            pl.BlockSpec((tm, tn), lambda j, i: (i, j)),
            pl.BlockSpec((tm, tn), lambda j, i: (i, nj + j)),
            pl.BlockSpec((tm, tn), lambda j, i: (i, 2 * nj + j)),
            wspec(d), wspec(kv), wspec(ku),
        ],
        out_specs=pl.BlockSpec((tm, tn), lambda j, i: (i, j)),
        out_shape=jax.ShapeDtypeStruct((t, d), BF16),
        scratch_shapes=[pltpu.VMEM((d, tn), BF16), pltpu.VMEM((kv, tn), BF16),
                        pltpu.VMEM((ku, tn), BF16)],
        compiler_params=_cparams(2, 48),
    )(y, v, u, gates, gates, gates, ssd_out, sc_out, cf_out)


def _wo_kernel(m_ref, w_ref, xa_ref, xb_ref, g_ref, o_ref, wb, *, tm, geo):
    i = pl.program_id(1)

    @pl.when(i == 0)
    def _():
        wb[...] = w_ref[...].astype(BF16)

    row = _mod_row(i * tm, *geo)
    out = g_ref[pl.ds(row, 1), :] * jnp.dot(m_ref[...], wb[...], preferred_element_type=F32)

    @pl.when(i * tm < geo[0])
    def _():
        o_ref[...] = xa_ref[...] + out

    @pl.when(i * tm >= geo[0])
    def _():
        o_ref[...] = xb_ref[...] + out


def _wo(merged, w_o, x_parts, mod, l, geo):
    t, d = merged.shape
    r = mod.shape[1]
    tm, tn = 512, min(d, 1024)
    nj = d // tn
    return pl.pallas_call(
        functools.partial(_wo_kernel, tm=tm, geo=geo),
        grid=(nj, t // tm),
        in_specs=[
            pl.BlockSpec((tm, d), lambda j, i: (i, 0)),
            pl.BlockSpec((None, d, tn), lambda j, i: (l, 0, j), pipeline_mode=pl.Buffered(1)),
        ] + _split_specs(x_parts, tm, tn, geo[0] // tm, 1, 0) + [
            pl.BlockSpec((None, r, tn), lambda j, i: (l, 0, 2 * nj + j)),
        ],
        out_specs=pl.BlockSpec((tm, tn), lambda j, i: (i, j)),
        out_shape=jax.ShapeDtypeStruct((t, d), F32),
        scratch_shapes=[pltpu.VMEM((d, tn), BF16)],
        compiler_params=_cparams(2, VMEM_BUDGET_MIB),
    )(merged, w_o, *x_parts, mod)


def _router_kernel(x_ref, g_ref, sh_ref, s_ref, rw_ref, rb_ref,
                   hp_ref, idx_ref, wt_ref, rank_ref, cnt_ref, carry, *, tm, geo, n_exp):
    i = pl.program_id(0)

    @pl.when(i == 0)
    def _():
        carry[...] = jnp.zeros(carry.shape, F32)

    row = _mod_row(i * tm, *geo)
    x = x_ref[...]
    y = x * lax.rsqrt(jnp.mean(x * x, axis=-1, keepdims=True) + EPS) * g_ref[...]
    h = y * (1.0 + s_ref[pl.ds(row, 1), :]) + sh_ref[pl.ds(row, 1), :]
    half = h.shape[1] // 2
    lo = lax.bitcast_convert_type(h[:, :half].astype(BF16).astype(F32), U32)
    hi = lax.bitcast_convert_type(h[:, half:].astype(BF16).astype(F32), U32)
    hp_ref[...] = (lo >> 16) | (hi & jnp.uint32(0xFFFF0000))

    logits = _dot_split(h, rw_ref[...])
    lt = logits.T[0:n_exp, :]
    score = _sigmoid(lt)
    sel = score + rb_ref[...]
    srow = [sel[e:e + 1, :] for e in range(n_exp)]
    prow = [score[e:e + 1, :] for e in range(n_exp)]

    def group_score(gi):
        v = srow[gi * GROUP_SIZE:(gi + 1) * GROUP_SIZE]
        best = None
        for a in range(GROUP_SIZE):
            for b in range(a + 1, GROUP_SIZE):
                s = v[a] + v[b]
                best = s if best is None else jnp.maximum(best, s)
        return best

    best_g = group_score(0)
    bi = jnp.zeros(best_g.shape, I32)
    for gi in range(1, N_EXPERT_GROUPS):
        gs = group_score(gi)
        upd = gs > best_g
        best_g = jnp.where(upd, gs, best_g)
        bi = jnp.where(upd, gi, bi)

    def pick(rows, j):
        out = rows[j]
        for gi in range(1, N_EXPERT_GROUPS):
            out = jnp.where(bi == gi, rows[gi * GROUP_SIZE + j], out)
        return out

    v = [pick(srow, j) for j in range(GROUP_SIZE)]
    p = [pick(prow, j) for j in range(GROUP_SIZE)]

    def argmax_first(vals):
        best, arg = vals[0], jnp.zeros(vals[0].shape, I32)
        for j in range(1, GROUP_SIZE):
            upd = vals[j] > best
            best = jnp.where(upd, vals[j], best)
            arg = jnp.where(upd, j, arg)
        return arg

    i1 = argmax_first(v)
    i2 = argmax_first([jnp.where(i1 == j, -jnp.inf, v[j]) for j in range(GROUP_SIZE)])
    p1 = sum(jnp.where(i1 == j, p[j], 0.0) for j in range(GROUP_SIZE))
    p2 = sum(jnp.where(i2 == j, p[j], 0.0) for j in range(GROUP_SIZE))
    den = p1 + p2
    e1 = bi * GROUP_SIZE + i1
    e2 = bi * GROUP_SIZE + i2
    idx_ref[0:1, :] = e1
    idx_ref[1:2, :] = e2
    wt_ref[0:1, :] = p1 / den
    wt_ref[1:2, :] = p2 / den

    erow = lax.broadcasted_iota(I32, (n_exp, tm), 0)
    oh1 = (erow == e1).astype(F32)
    oh2 = (erow == e2).astype(F32)
    cnt = oh1 + oh2
    before = (lax.broadcasted_iota(I32, (tm, tm), 0)
              < lax.broadcasted_iota(I32, (tm, tm), 1)).astype(BF16)
    prior = jnp.dot(cnt.astype(BF16), before, preferred_element_type=F32) + carry[...]
    rank_ref[0:1, :] = jnp.sum(oh1 * prior, axis=0, keepdims=True).astype(I32)
    rank_ref[1:2, :] = jnp.sum(oh2 * prior, axis=0, keepdims=True).astype(I32)
    carry[...] = carry[...] + jnp.sum(cnt, axis=1, keepdims=True)
    cnt_ref[...] = jnp.broadcast_to(carry[...], cnt_ref.shape)


def _router(x, g, mod, l, rw_pad, rb_col, geo):
    t, d = x.shape
    r = mod.shape[1]
    n_exp = rb_col.shape[0]
    tm = 256
    return pl.pallas_call(
        functools.partial(_router_kernel, tm=tm, geo=geo, n_exp=n_exp),
        grid=(t // tm,),
        in_specs=[
            pl.BlockSpec((tm, d), lambda i: (i, 0)),
            pl.BlockSpec((1, d), lambda i: (0, 0)),
            pl.BlockSpec((None, r, d), lambda i: (l, 0, 3)),
            pl.BlockSpec((None, r, d), lambda i: (l, 0, 4)),
            pl.BlockSpec(rw_pad.shape, lambda i: (0, 0)),
            pl.BlockSpec(rb_col.shape, lambda i: (0, 0)),
        ],
        out_specs=[
            pl.BlockSpec((tm, d // 2), lambda i: (i, 0)),
            pl.BlockSpec((2, tm), lambda i: (0, i)),
            pl.BlockSpec((2, tm), lambda i: (0, i)),
            pl.BlockSpec((2, tm), lambda i: (0, i)),
            pl.BlockSpec((n_exp, 128), lambda i: (0, 0)),
        ],
        out_shape=[
            jax.ShapeDtypeStruct((t, d // 2), U32),
            jax.ShapeDtypeStruct((2, t), I32),
            jax.ShapeDtypeStruct((2, t), F32),
            jax.ShapeDtypeStruct((2, t), I32),
            jax.ShapeDtypeStruct((n_exp, 128), F32),
        ],
        scratch_shapes=[pltpu.VMEM((n_exp, 1), F32)],
        compiler_params=_cparams(1, 32),
    )(x, g, mod, mod, rw_pad, rb_col)


def _row_copy(src, dst, sem):
    return pltpu.make_async_copy(src, dst, sem)


ZERO_ROWS = 256


def _dispatch_kernel(idx_s, rank_s, off_s, pad_s, h_ref, xs_out, sem, zbuf, zsem, *,
                     tm, t, n_exp, tile, n_tiles):
    base = pl.program_id(0) * tm

    @pl.when(pl.program_id(0) == 0)
    def _():
        zbuf[...] = jnp.zeros(zbuf.shape, U32)

        def for_each_clear(fn):
            for e in range(n_exp):
                start, n = pad_s[e], pad_s[n_exp + e]
                head = (-start) & 7
                for q in range(7):
                    cp = _row_copy(zbuf.at[pl.ds(0, 1)], xs_out.at[pl.ds(start + q, 1)], zsem)
                    pl.when(q < head)(functools.partial(fn, cp))
                pos = pl.multiple_of(start + head, 8)
                rest = n - head
                p = ZERO_ROWS
                while p >= 8:
                    cp = _row_copy(zbuf.at[pl.ds(0, p)], xs_out.at[pl.ds(pos, p)], zsem)
                    pl.when((rest & p) != 0)(functools.partial(fn, cp))
                    pos = pl.multiple_of(pos + (rest & p), 8)
                    p //= 2

            def unused(tl, carry):
                for part in range(tile // ZERO_ROWS):
                    row = pl.multiple_of(tl * tile + part * ZERO_ROWS, ZERO_ROWS)
                    fn(_row_copy(zbuf, xs_out.at[pl.ds(row, ZERO_ROWS)], zsem))
                return carry

            lax.fori_loop(pad_s[2 * n_exp], n_tiles, unused, 0)

        for_each_clear(lambda cp: cp.start())
        for_each_clear(lambda cp: cp.wait())

    def issue(r, carry):
        for k in range(2):
            pos = k * t + base + r
            dst = off_s[idx_s[pos]] + rank_s[pos]
            _row_copy(h_ref.at[pl.ds(r, 1)], xs_out.at[pl.ds(dst, 1)], sem).start()
        return carry

    lax.fori_loop(0, tm, issue, 0, unroll=4)

    def drain(r, carry):
        for k in range(2):
            _row_copy(h_ref.at[pl.ds(0, 1)], xs_out.at[pl.ds(0, 1)], sem).wait()
        return carry

    lax.fori_loop(0, tm, drain, 0, unroll=4)


def _dispatch(hp, idx_flat, rank_flat, off, pad_meta, n_rows, n_exp, tile):
    t, w = hp.shape
    tm = 256
    return pl.pallas_call(
        functools.partial(_dispatch_kernel, tm=tm, t=t, n_exp=n_exp, tile=tile,
                          n_tiles=n_rows // tile),
        grid_spec=pltpu.PrefetchScalarGridSpec(
            num_scalar_prefetch=4,
            grid=(t // tm,),
            in_specs=[pl.BlockSpec((tm, w), lambda i, *_: (i, 0))],
            out_specs=pl.BlockSpec(memory_space=pl.ANY),
            scratch_shapes=[pltpu.SemaphoreType.DMA(()), pltpu.VMEM((ZERO_ROWS, w), U32),
                            pltpu.SemaphoreType.DMA(())],
        ),
        out_shape=jax.ShapeDtypeStruct((n_rows, w), U32),
        compiler_params=_cparams(1, 32),
    )(idx_flat, rank_flat, off, pad_meta, hp)


def _expert_changed(te_s, i):
    return (i == 0) | (te_s[i] != te_s[jnp.maximum(i - 1, 0)])


def _up_kernel(te_s, nu_s, x_ref, wg_ref, wu_ref, o_ref, wgb, wub):
    i = pl.program_id(1)

    @pl.when(_expert_changed(te_s, i))
    def _():
        wgb[...] = wg_ref[...].astype(BF16)
        wub[...] = wu_ref[...].astype(BF16)

    @pl.when(i < nu_s[0])
    def _():
        pk = x_ref[...]
        lo = lax.bitcast_convert_type(pk << 16, F32)
        hi = lax.bitcast_convert_type(pk & jnp.uint32(0xFFFF0000), F32)
        x = jnp.concatenate([lo, hi], axis=1).astype(BF16)
        gt = jnp.dot(x, wgb[...], preferred_element_type=F32)
        up = jnp.dot(x, wub[...], preferred_element_type=F32)
        o_ref[...] = (_silu(gt) * up).astype(BF16)

    @pl.when(i >= nu_s[0])
    def _():
        o_ref[...] = jnp.zeros(o_ref.shape, BF16)


def _moe_up(xs, te, nu, w_gate, w_up, l, tm):
    n_rows, w = xs.shape
    d, f = w_gate.shape[2], w_gate.shape[3]
    tf = 512
    wspec = pl.BlockSpec((None, None, d, tf), lambda j, i, te_s, nu_s: (l, te_s[i], 0, j))
    return pl.pallas_call(
        _up_kernel,
        grid_spec=pltpu.PrefetchScalarGridSpec(
            num_scalar_prefetch=2,
            grid=(f // tf, n_rows // tm),
            in_specs=[pl.BlockSpec((tm, w), lambda j, i, *_: (i, 0)), wspec, wspec],
            out_specs=pl.BlockSpec((tm, tf), lambda j, i, *_: (i, j)),
            scratch_shapes=[pltpu.VMEM((d, tf), BF16)] * 2,
        ),
        out_shape=jax.ShapeDtypeStruct((n_rows, f), BF16),
        compiler_params=_cparams(2, VMEM_BUDGET_MIB),
    )(te, nu, xs, w_gate, w_up)


def _down_kernel(te_s, nu_s, a_ref, w_ref, o_ref, wb):
    i = pl.program_id(1)

    @pl.when(_expert_changed(te_s, i))
    def _():
        wb[...] = w_ref[...].astype(BF16)

    @pl.when(i < nu_s[0])
    def _():
        o_ref[...] = jnp.dot(a_ref[...], wb[...], preferred_element_type=F32)

    @pl.when(i >= nu_s[0])
    def _():
        o_ref[...] = jnp.zeros(o_ref.shape, F32)


def _moe_down(act, te, nu, w_down, l, tm):
    n_rows, f = act.shape
    d = w_down.shape[3]
    tn = min(d, 2048)
    return pl.pallas_call(
        _down_kernel,
        grid_spec=pltpu.PrefetchScalarGridSpec(
            num_scalar_prefetch=2,
            grid=(d // tn, n_rows // tm),
            in_specs=[pl.BlockSpec((tm, f), lambda j, i, *_: (i, 0)),
                      pl.BlockSpec((None, None, f, tn), lambda j, i, te_s, nu_s: (l, te_s[i], 0, j))],
            out_specs=pl.BlockSpec((tm, tn), lambda j, i, *_: (i, j)),
            scratch_shapes=[pltpu.VMEM((f, tn), BF16)],
        ),
        out_shape=jax.ShapeDtypeStruct((n_rows, d), F32),
        compiler_params=_cparams(2, 48),
    )(te, nu, act, w_down)


def _combine_kernel(idx_s, rank_s, off_s, x_ref, wt_ref, g2_ref, ng_ref, sh_ref, s_ref, ys_ref,
                    *rest, tm, t, geo, final):
    o_ref, o2_ref, buf, sem = rest
    i = pl.program_id(0)
    n_steps = t // tm
    slot = lax.rem(i, 2)

    def gather(tile, dst_slot):
        def issue(r, carry):
            for k in range(2):
                pos = k * t + tile * tm + r
                src = off_s[idx_s[pos]] + rank_s[pos]
                _row_copy(ys_ref.at[pl.ds(src, 1)], buf.at[dst_slot, k, pl.ds(r, 1)],
                          sem.at[dst_slot]).start()
            return carry

        lax.fori_loop(0, tm, issue, 0, unroll=4)

    @pl.when(i == 0)
    def _():
        gather(0, 0)

    def drain(which):
        def body(r, carry):
            for k in range(2):
                _row_copy(ys_ref.at[pl.ds(0, 1)], buf.at[which, k, pl.ds(0, 1)],
                          sem.at[which]).wait()
            return carry

        lax.fori_loop(0, tm, body, 0, unroll=4)

    drain(slot)

    nxt = jnp.where(i + 1 < n_steps, i + 1, 0)
    row = _mod_row(i * tm, *geo)
    n_chunks = 8
    cr = tm // n_chunks

    def process(emit):
        for c in range(n_chunks):
            for r in range(c * cr, (c + 1) * cr):
                for k in range(2):
                    pos = k * t + nxt * tm + r
                    src = off_s[idx_s[pos]] + rank_s[pos]
                    _row_copy(ys_ref.at[pl.ds(src, 1)], buf.at[1 - slot, k, pl.ds(r, 1)],
                              sem.at[1 - slot]).start()
            rs = slice(c * cr, (c + 1) * cr)
            wt = wt_ref[rs, :]
            moe = wt[:, 0:1] * buf[slot, 0, rs, :] + wt[:, 1:2] * buf[slot, 1, rs, :]
            xn = x_ref[rs, :] + g2_ref[pl.ds(row, 1), :] * moe
            y = xn * lax.rsqrt(jnp.mean(xn * xn, axis=-1, keepdims=True) + EPS) * ng_ref[...]
            emit(rs, xn, y)

    if final:
        def emit_into(ref):
            def emit(rs, xn, y):
                ref[rs, :] = y
            return emit

        pl.when(i * tm < geo[0])(lambda: process(emit_into(o_ref)))
        pl.when(i * tm >= geo[0])(lambda: process(emit_into(o2_ref)))
    else:
        def emit(rs, xn, y):
            o_ref[rs, :] = xn
            o2_ref[rs, :] = (y * (1.0 + s_ref[pl.ds(row, 1), :])
                             + sh_ref[pl.ds(row, 1), :]).astype(BF16)

        process(emit)

    @pl.when(i + 1 >= n_steps)
    def _():
        drain(1 - slot)


def _combine(x, wt_col, mod, l, norm_g, ys, idx_flat, rank_flat, off, geo, final):
    t, d = x.shape
    r = mod.shape[1]
    tm = 256
    l_next = l if final else l + 1
    if final:
        n_ctx_tiles = geo[0] // tm
        out_specs = [pl.BlockSpec((tm, d), lambda i, *_: (jnp.minimum(i, n_ctx_tiles - 1), 0)),
                     pl.BlockSpec((tm, d), lambda i, *_: (jnp.maximum(i - n_ctx_tiles, 0), 0))]
        out_shape = [jax.ShapeDtypeStruct((geo[0], d), F32),
                     jax.ShapeDtypeStruct((t - geo[0], d), F32)]
    else:
        out_specs = [pl.BlockSpec((tm, d), lambda i, *_: (i, 0))] * 2
        out_shape = [jax.ShapeDtypeStruct((t, d), F32), jax.ShapeDtypeStruct((t, d), BF16)]
    return pl.pallas_call(
        functools.partial(_combine_kernel, tm=tm, t=t, geo=geo, final=final),
        grid_spec=pltpu.PrefetchScalarGridSpec(
            num_scalar_prefetch=3,
            grid=(t // tm,),
            in_specs=[
                pl.BlockSpec((tm, d), lambda i, *_: (i, 0)),
                pl.BlockSpec((tm, 2), lambda i, *_: (i, 0)),
                pl.BlockSpec((None, r, d), lambda i, *_: (l, 0, 5)),
                pl.BlockSpec((1, d), lambda i, *_: (0, 0)),
                pl.BlockSpec((None, r, d), lambda i, *_: (l_next, 0, 0)),
                pl.BlockSpec((None, r, d), lambda i, *_: (l_next, 0, 1)),
                pl.BlockSpec(memory_space=pl.ANY),
            ],
            out_specs=out_specs,
            scratch_shapes=[pltpu.VMEM((2, 2, tm, d), F32), pltpu.SemaphoreType.DMA((2,))],
        ),
        out_shape=out_shape,
        compiler_params=_cparams(1, 48),
    )(idx_flat, rank_flat, off, x, wt_col, mod, norm_g, mod, mod, ys)


def kernel(x_prompt, x_sample, state_ssd, c, c_ctx, ada_w, ada_b, norm1_g, norm2_g, w_in,
           ssd_conv_w, ssd_conv_b, ssd_dt_bias, ssd_a_log, ssd_d, ssd_norm_g, ssd_out,
           sc_conv_w, sc_conv_b, sc_out, cf_conv_w, cf_conv_b, cf_ln_g, cf_ln_b, cf_out, w_o,
           router_w, router_bias, moe_w_gate, moe_w_up, moe_w_down, final_g):
    n_ctx, seq, d = x_prompt.shape
    n_dec, dec_seq, _ = x_sample.shape
    depth = w_in.shape[0]
    n_heads = d // HEAD_DIM
    gn = SSD_GROUPS * SSD_STATE
    xbc_w = d + 2 * gn
    sc_w = sc_out.shape[1]
    cf_w = cf_out.shape[1]
    n_exp = router_w.shape[1]
    t_ctx = n_ctx * seq
    t = t_ctx + n_dec * dec_seq
    geo = (t_ctx, dec_seq, n_dec)
    off_xbc = d
    off_dt = off_xbc + xbc_w
    off_sc = off_dt + 2 * n_heads
    off_cf = off_sc + 3 * sc_w
    off_gate = off_cf + 2 * cf_w

    x_parts = (x_prompt.reshape(t_ctx, d), x_sample.reshape(n_dec * dec_seq, d))
    n_mod_rows = -(-(n_dec + 1) // 8) * 8
    cond = jnp.concatenate([c, c_ctx[None, :],
                            jnp.zeros((n_mod_rows - n_dec - 1, d), F32)], axis=0)
    mod = _ada_table(cond, ada_w, ada_b)

    h0_lat = state_ssd.reshape(n_dec, depth, 2, n_heads // 2, PAIR, SSD_STATE)
    state_buf = jnp.zeros((n_ctx, depth, 2, n_heads // 2, PAIR, SSD_STATE), F32)
    rw_pad = jnp.pad(router_w, ((0, 0), (0, 128 - n_exp)))
    rb_col = router_bias.reshape(n_exp, 1)
    tm_e = 512
    n_rows = 2 * t + n_exp * tm_e
    n_tiles = n_rows // tm_e

    h = _normmod(x_parts, norm1_g[0:1], mod, 0, geo)
    y_final = None
    for l in range(depth):
        z = _proj(h, w_in, l, [0], d, 512, 1024, _epi_identity, [], [], F32)
        xbc = _proj(
            h, w_in, l, [off_xbc], xbc_w, 512, dec_seq,
            functools.partial(_epi_xbc, tm=dec_seq, t_ctx=t_ctx, seq=seq, dec_seq=dec_seq),
            [ssd_conv_w, ssd_conv_b.reshape(depth, 1, xbc_w)],
            [pl.BlockSpec((None, 3, 512), lambda j, i: (l, 0, j)),
             pl.BlockSpec((None, 1, 512), lambda j, i: (l, 0, j))], F32)
        dt = _proj(h, w_in, l, [off_dt], 2 * n_heads, 2 * n_heads, 1024, _epi_dt,
                   [ssd_dt_bias.reshape(depth, 1, 2 * n_heads)],
                   [pl.BlockSpec((None, 1, 2 * n_heads), lambda j, i: (l, 0, 0))], F32)
        v = _proj(
            h, w_in, l, [off_sc, off_sc + sc_w, off_sc + 2 * sc_w], sc_w, 256, 1024,
            functools.partial(_epi_sc, tm=1024, t_ctx=t_ctx, seq=seq),
            [sc_conv_w, sc_conv_b.reshape(depth, 1, sc_w)],
            [pl.BlockSpec((None, 3, 256), lambda j, i: (l, 0, j)),
             pl.BlockSpec((None, 1, 256), lambda j, i: (l, 0, j))], BF16)
        u_cf = _proj(h, w_in, l, [off_cf, off_cf + cf_w], cf_w, 256, 1024, _epi_cf, [], [], F32)
        gates = _proj(h, w_in, l, [off_gate], 3 * d, 512, 1024, _epi_sigmoid, [], [], BF16)

        alog = ssd_a_log[l].reshape(1, 2 * n_heads)
        dskip = jnp.repeat(ssd_d[l], HEAD_DIM).reshape(1, d)
        ng = ssd_norm_g[l:l + 1]
        scan = dict(d=d, l=l, shape=(n_ctx, seq, n_dec, dec_seq))
        yf, state_buf = _ssd_pass(xbc, dt, alog, h0_lat, state_buf, reverse=False, **scan)
        y_ssd, state_buf = _ssd_pass(xbc, dt, alog, h0_lat, state_buf, reverse=True,
                                     fused_in=(yf, z, dskip, ng), **scan)

        u = _cfconv(u_cf, cf_conv_w[l], cf_conv_b[l:l + 1], cf_ln_g[l:l + 1], cf_ln_b[l:l + 1],
                    t_ctx=t_ctx, seq=seq, dec_seq=dec_seq)

        merged = _merge(y_ssd, v, u, gates, ssd_out, sc_out, cf_out, l)
        x = _wo(merged, w_o, x_parts, mod, l, geo)

        hp, idx, wt, rank, cnt = _router(x, norm2_g[l:l + 1], mod, l, rw_pad, rb_col, geo)
        counts = cnt[:, 0].astype(I32)
        tiles = (counts + tm_e - 1) // tm_e
        tile_end = jnp.cumsum(tiles)
        off = (tile_end - tiles) * tm_e
        n_used = tile_end[-1]
        tile_id = jnp.minimum(jnp.arange(n_tiles, dtype=I32), n_used - 1)
        te = jnp.sum(tile_id[:, None] >= tile_end[None, :], axis=1).astype(I32)
        nu = n_used.reshape(1).astype(I32)
        idx_flat = idx.reshape(2 * t)
        rank_flat = rank.reshape(2 * t)
        pad_meta = jnp.concatenate([off + counts, tiles * tm_e - counts, nu]).astype(I32)
        xs = _dispatch(hp, idx_flat, rank_flat, off, pad_meta, n_rows, n_exp, tm_e)
        act = _moe_up(xs, te, nu, moe_w_gate, moe_w_up, l, tm_e)
        ys = _moe_down(act, te, nu, moe_w_down, l, tm_e)
        final = l == depth - 1
        ng_next = final_g.reshape(1, d) if final else norm1_g[l + 1:l + 2]
        outs = _combine(x, wt.T, mod, l, ng_next, ys, idx_flat, rank_flat, off, geo, final)
        if final:
            y_final = outs
        else:
            x, h = outs
            x_parts = (x, x)

    y_prompt = y_final[0].reshape(n_ctx, seq, d)
    y_sample = y_final[1].reshape(n_dec, dec_seq, d)
    new_state = state_buf.reshape(n_ctx, depth, 2, n_heads, HEAD_DIM, SSD_STATE)
    return (y_prompt, y_sample, new_state)
```
